```python
import jax, jax.numpy as jnp
from jax import lax
import numpy as np

D_MODEL = 1024
BATCH = 2
SEQ = 8192
DEPTH = 2

N_META = 16
FOX_HEADS = 16
FOX_HEAD_DIM = D_MODEL // FOX_HEADS
Q_BLOCK = 128
HGRN_EXPAND = 128
HGRN_HEADS = D_MODEL // HGRN_EXPAND
HGRN_CHUNK = 64
D_FF = ((8 * D_MODEL // 3 + 127) // 128) * 128
CONV_WIDTH = 3
EPS = 1e-6
N_FOX_LAYERS = (DEPTH + 1) // 2
N_HGRN_LAYERS = DEPTH // 2

kernel_name = "fox_hgrn2_interleaved_convffn"


def rms_norm(x, gain):
    x32 = x.astype(jnp.float32)
    y = x32 * lax.rsqrt(jnp.mean(x32 * x32, axis=-1, keepdims=True) + EPS)
    return (y * gain.astype(jnp.float32)).astype(x.dtype)


def fox_attention(h, norm_g, w_in, b_f, q_gain, k_gain, w_out):
    B, L, D = h.shape
    H, hd = FOX_HEADS, FOX_HEAD_DIM
    xn = rms_norm(h, norm_g)
    q, k, v, f_logit, o_gate = jnp.split(xn @ w_in, [D, 2 * D, 3 * D, 3 * D + H], axis=-1)
    q = rms_norm(q.reshape(B, L, H, hd), q_gain).astype(jnp.float32) * (hd ** -0.5)
    k = rms_norm(k.reshape(B, L, H, hd), k_gain).astype(jnp.float32)
    v = v.reshape(B, L, H, hd)
    log_f = jax.nn.log_sigmoid((f_logit + b_f).astype(jnp.float32))
    c = jnp.cumsum(log_f, axis=1).transpose(0, 2, 1)
    key_pos = jnp.arange(L)

    def attend(q_blk, c_blk, q_pos):
        s = jnp.einsum('bqhd,bkhd->bhqk', q_blk, k)
        s = s + c_blk[..., None] - c[:, :, None, :]
        s = jnp.where(q_pos[:, None] >= key_pos[None, :], s, -jnp.inf)
        p = jax.nn.softmax(s, axis=-1)
        return jnp.einsum('bhqk,bkhd->bqhd', p.astype(v.dtype), v)

    o_meta = attend(q[:, :N_META], c[:, :, :N_META], jnp.arange(N_META))
    n_blk = (L - N_META) // Q_BLOCK
    q_r = q[:, N_META:].reshape(B, n_blk, Q_BLOCK, H, hd).transpose(1, 0, 2, 3, 4)
    c_r = c[:, :, N_META:].reshape(B, H, n_blk, Q_BLOCK).transpose(2, 0, 1, 3)
    pos_r = (N_META + jnp.arange(L - N_META)).reshape(n_blk, Q_BLOCK)
    o_r = lax.map(lambda a: attend(*a), (q_r, c_r, pos_r))
    o_r = o_r.transpose(1, 0, 2, 3, 4).reshape(B, L - N_META, H, hd)
    o = jnp.concatenate([o_meta, o_r], axis=1).reshape(B, L, D)
    o = o * jax.nn.sigmoid(o_gate)
    return o @ w_out


def hgrn2_chunk(S, q, k, v, log_f):
    C = q.shape[2]
    b = jnp.cumsum(log_f, axis=2)
    causal = jnp.arange(C)[:, None] >= jnp.arange(C)[None, :]
    diff = b[:, :, :, None, :] - b[:, :, None, :, :]
    decay = jnp.exp(jnp.where(causal[None, None, :, :, None], diff, -jnp.inf))
    scores = jnp.einsum('bhtk,bhtsk,bhsk->bhts', q, decay, k)
    o = jnp.einsum('bhts,bhsv->bhtv', scores, v) + jnp.einsum('bhtk,bhkv->bhtv', q * jnp.exp(b), S)
    b_end = b[:, :, -1]
    S_new = jnp.exp(b_end)[..., None] * S + jnp.einsum(
        'bhsk,bhsv->bhkv', k * jnp.exp(b_end[:, :, None, :] - b), v)
    return S_new, o


def hgrn2_mixer(h, norm_g, w_in, lower_bound, o_gain, w_out):
    B, L, D = h.shape
    H, dk = HGRN_HEADS, HGRN_EXPAND
    xn = rms_norm(h, norm_g)
    q, f_logit, i, g = jnp.split(xn @ w_in, 4, axis=-1)
    lb = lower_bound.astype(jnp.float32)
    f = lb + (1.0 - lb) * jax.nn.sigmoid(f_logit.astype(jnp.float32))

    def to_heads(t):
        return t.astype(jnp.float32).reshape(B, L, H, dk).transpose(0, 2, 1, 3)

    q = to_heads(jax.nn.silu(q))
    k = to_heads(1.0 - f)
    log_f = to_heads(jnp.log(f))
    v = to_heads(i)
    S0 = jnp.zeros((B, H, dk, dk), jnp.float32)
    S, o_meta = hgrn2_chunk(S0, q[:, :, :N_META], k[:, :, :N_META], v[:, :, :N_META], log_f[:, :, :N_META])
    n_chunk = (L - N_META) // HGRN_CHUNK

    def chunks(t):
        return t[:, :, N_META:].reshape(B, H, n_chunk, HGRN_CHUNK, dk).transpose(2, 0, 1, 3, 4)

    _, o_r = lax.scan(lambda s, a: hgrn2_chunk(s, *a), S, (chunks(q), chunks(k), chunks(v), chunks(log_f)))
    o_r = o_r.transpose(1, 2, 0, 3, 4).reshape(B, H, L - N_META, dk)
    o = jnp.concatenate([o_meta, o_r], axis=2).transpose(0, 2, 1, 3)
    o = rms_norm(o, o_gain).reshape(B, L, D) * jax.nn.sigmoid(g.astype(jnp.float32))
    return o.astype(h.dtype) @ w_out


def conv_ffn(h, norm_g, w_gate, w_up, conv_w, conv_b, w_down):
    xn = rms_norm(h, norm_g)
    a = xn @ w_gate
    L = a.shape[1]
    a_pad = jnp.pad(a, ((0, 0), (CONV_WIDTH - 1, 0), (0, 0)))
    a = (conv_b + conv_w[0] * a_pad[:, 0:L] + conv_w[1] * a_pad[:, 1:L + 1]
         + conv_w[2] * a_pad[:, 2:L + 2])
    return (jax.nn.silu(a) * (xn @ w_up)) @ w_down


def setup_inputs(seed: int = 0) -> dict:
    key = jax.random.key(seed)
    ks = jax.random.split(key, 19)
    D, F, H = D_MODEL, D_FF, FOX_HEADS
    nF, nH = N_FOX_LAYERS, N_HGRN_LAYERS

    def normal(k, shape, scale):
        return scale * jax.random.normal(k, shape, jnp.float32)

    def gain(k, shape):
        return 1.0 + normal(k, shape, 0.05)

    return {
        "x": normal(ks[0], (BATCH, SEQ, D), 1.0),
        "meta_tokens": normal(ks[1], (N_META, D), 1.0),
        "fox_norm": gain(ks[2], (nF, D)),
        "fox_w_in": normal(ks[3], (nF, D, 4 * D + H), D ** -0.5),
        "fox_b_f": 2.0 + normal(ks[4], (nF, H), 0.1),
        "fox_q_gain": gain(ks[5], (nF, FOX_HEAD_DIM)),
        "fox_k_gain": gain(ks[6], (nF, FOX_HEAD_DIM)),
        "fox_w_out": normal(ks[7], (nF, D, D), D ** -0.5),
        "hgrn_norm": gain(ks[8], (nH, D)),
        "hgrn_w_in": normal(ks[9], (nH, D, 4 * D), D ** -0.5),
        "hgrn_lower_bounds": normal(ks[10], (DEPTH, D), 0.1),
        "hgrn_o_gain": gain(ks[11], (nH, HGRN_EXPAND)),
        "hgrn_w_out": normal(ks[12], (nH, D, D), D ** -0.5),
        "ffn_norm": gain(ks[13], (DEPTH, D)),
        "ffn_w_gate": normal(ks[14], (DEPTH, D, F), D ** -0.5),
        "ffn_w_up": normal(ks[15], (DEPTH, D, F), D ** -0.5),
        "ffn_conv_w": normal(ks[16], (DEPTH, CONV_WIDTH, F), CONV_WIDTH ** -0.5),
        "ffn_conv_b": normal(ks[17], (DEPTH, F), 0.02),
        "ffn_w_down": normal(ks[18], (DEPTH, F, D), F ** -0.5),
    }


def reference(x, meta_tokens, fox_norm, fox_w_in, fox_b_f, fox_q_gain, fox_k_gain, fox_w_out,
              hgrn_norm, hgrn_w_in, hgrn_lower_bounds, hgrn_o_gain, hgrn_w_out,
              ffn_norm, ffn_w_gate, ffn_w_up, ffn_conv_w, ffn_conv_b, ffn_w_down):
    B = x.shape[0]
    meta = jnp.broadcast_to(meta_tokens[None].astype(x.dtype), (B, N_META, D_MODEL))
    h = jnp.concatenate([meta, x], axis=1)
    p_lb = jax.nn.softmax(hgrn_lower_bounds.astype(jnp.float32), axis=0)
    lower_bounds = jnp.cumsum(p_lb, axis=0) - p_lb[0]
    for layer in range(DEPTH):
        j = layer // 2
        if layer % 2 == 0:
            h = h + fox_attention(h, fox_norm[j], fox_w_in[j], fox_b_f[j], fox_q_gain[j],
                                  fox_k_gain[j], fox_w_out[j])
        else:
            h = h + hgrn2_mixer(h, hgrn_norm[j], hgrn_w_in[j], lower_bounds[layer],
                                hgrn_o_gain[j], hgrn_w_out[j])
        h = h + conv_ffn(h, ffn_norm[layer], ffn_w_gate[layer], ffn_w_up[layer],
                         ffn_conv_w[layer], ffn_conv_b[layer], ffn_w_down[layer])
    return h[:, N_META:]
```

```python
import functools

import jax
import jax.numpy as jnp
from jax import lax
from jax.experimental import pallas as pl
from jax.experimental.pallas import tpu as pltpu

N_META = 16
EPS = 1e-6
NEG = -1e30
LANE = 128
SUBLANE = 8
TOK = 256
ROW_TILE = 512
FFN_CHUNK = 256
HGRN_CHUNK = 64
HGRN_SUB = 16
N_EXTRA = 16
VMEM_LIMIT = 56 * 1024 * 1024

F32 = jnp.float32
BF16 = jnp.bfloat16


def _cparams(*sem):
    return pltpu.CompilerParams(dimension_semantics=sem, vmem_limit_bytes=VMEM_LIMIT)


def _resident(shape):
    zeros = (0,) * len(shape)
    return pl.BlockSpec(shape, lambda *_: zeros)


def _split3(x):
    hi = x.astype(BF16)
    r = x - hi.astype(F32)
    mid = r.astype(BF16)
    lo = (r - mid.astype(F32)).astype(BF16)
    return hi, mid, lo


def _dot(a, b):
    return jnp.dot(a, b, preferred_element_type=F32)


def _dot_nt(a, b):
    return lax.dot_general(a, b, (((1,), (1,)), ((), ())), preferred_element_type=F32)


def _dot_tn(a, b):
    return lax.dot_general(a, b, (((0,), (0,)), ((), ())), preferred_element_type=F32)


def _rms(x, gain):
    ms = jnp.mean(x * x, axis=-1, keepdims=True)
    return x * lax.rsqrt(ms + EPS) * gain


def _fox_in_kernel(h_ref, g_ref, w_ref, wf_ref, bf_ref, qg_ref, kg_ref,
                   qt_ref, k_ref, vt_ref, gt_ref, carry_ref, *, heads):
    @pl.when(pl.program_id(1) == 0)
    def _():
        carry_ref[...] = jnp.zeros_like(carry_ref)

    T, D = h_ref.shape[1], h_ref.shape[2]
    hd = D // heads
    xnt = _rms(h_ref[0], g_ref[...]).T.astype(BF16)

    def proj(i):
        return _dot(w_ref[i * D:(i + 1) * D, :], xnt)

    fl = _dot(wf_ref[...], xnt) + bf_ref[...]
    lf = jnp.minimum(fl, 0.0) - jnp.log(1.0 + jnp.exp(-jnp.abs(fl)))
    src = lax.broadcasted_iota(jnp.int32, (T, T), 0)
    dst = lax.broadcasted_iota(jnp.int32, (T, T), 1)
    upper = jnp.where(src <= dst, 1.0, 0.0).astype(BF16)
    hi, mid, lo = _split3(lf)
    c = carry_ref[:, 0:1] + (_dot(hi, upper) + _dot(mid, upper) + _dot(lo, upper))
    carry_ref[...] = carry_ref[...] + jnp.sum(lf, axis=1, keepdims=True)
    chi, cmid, clo = (p.astype(F32) for p in _split3(c))

    def head_norm(y, gain_ref, scale):
        y = y.reshape(heads, hd, T)
        ms = jnp.mean(y * y, axis=1, keepdims=True)
        return y * lax.rsqrt(ms + EPS) * (gain_ref[...] * scale)

    qn = head_norm(proj(0), qg_ref, hd ** -0.5)
    kn = head_norm(proj(1), kg_ref, 1.0)
    vv = proj(2).reshape(heads, hd, T)
    gt_ref[0] = jax.nn.sigmoid(proj(3)).astype(gt_ref.dtype)

    r = lax.broadcasted_iota(jnp.int32, (N_EXTRA, T), 0)
    zpad = jnp.zeros((LANE - hd - N_EXTRA, T), F32)
    ones_row = jnp.where(r == 0, 1.0, 0.0)
    for hh in range(heads):
        a, b, d = (jnp.broadcast_to(p[hh:hh + 1], (N_EXTRA, T)) for p in (chi, cmid, clo))
        eq = jnp.where(r == 0, a, jnp.where(r == 1, b, jnp.where(r == 2, d, jnp.where(r < 6, 1.0, 0.0))))
        ek = jnp.where(r < 3, 1.0, jnp.where(r == 3, -a, jnp.where(r == 4, -b, jnp.where(r == 5, -d, 0.0))))
        qt_ref[0, hh, 0] = jnp.concatenate([qn[hh], eq, zpad], axis=0).astype(qt_ref.dtype)
        k_ref[0, hh, 0] = jnp.concatenate([kn[hh], ek, zpad], axis=0).T.astype(k_ref.dtype)
        vt_ref[0, hh, 0] = jnp.concatenate([vv[hh], ones_row, zpad], axis=0).astype(vt_ref.dtype)


def _fox_in(h, gain, w_t, wf_t, b_f, q_gain, k_gain, heads):
    B, Lp, D = h.shape
    nT = Lp // TOK
    hd = D // heads
    blk_t = pl.BlockSpec((1, heads, 1, LANE, TOK), lambda b, t: (b, 0, t, 0, 0))
    return pl.pallas_call(
        functools.partial(_fox_in_kernel, heads=heads),
        grid=(B, nT),
        in_specs=[
            pl.BlockSpec((1, TOK, D), lambda b, t: (b, t, 0)),
            _resident((1, D)), _resident(w_t.shape), _resident((heads, D)),
            _resident((heads, 1)), _resident((hd, 1)), _resident((hd, 1)),
        ],
        out_specs=[
            blk_t,
            pl.BlockSpec((1, heads, 1, TOK, LANE), lambda b, t: (b, 0, t, 0, 0)),
            blk_t,
            pl.BlockSpec((1, D, TOK), lambda b, t: (b, 0, t)),
        ],
        out_shape=[
            jax.ShapeDtypeStruct((B, heads, nT, LANE, TOK), BF16),
            jax.ShapeDtypeStruct((B, heads, nT, TOK, LANE), BF16),
            jax.ShapeDtypeStruct((B, heads, nT, LANE, TOK), BF16),
            jax.ShapeDtypeStruct((B, D, Lp), BF16),
        ],
        scratch_shapes=[pltpu.VMEM((heads, LANE), F32)],
        compiler_params=_cparams("arbitrary", "arbitrary"),
        name="fox_in",
    )(h, gain.reshape(1, D), w_t, wf_t, b_f.reshape(heads, 1),
      q_gain.reshape(hd, 1), k_gain.reshape(hd, 1))


def _fox_attn_kernel(qt_ref, k_ref, vt_ref, gt_ref, o_ref, *, hd):
    qi = pl.program_id(2)
    T = qt_ref.shape[-1]
    pair = qt_ref.shape[1]

    def step(kb, carry, masked):
        out = []
        for j in range(pair):
            m, acc = carry[j]
            s = _dot(k_ref[0, j, kb], qt_ref[0, j, 0])
            if masked:
                key = lax.broadcasted_iota(jnp.int32, (T, T), 0)
                qry = lax.broadcasted_iota(jnp.int32, (T, T), 1)
                s = jnp.where(key <= qry, s, NEG)
            m_new = jnp.maximum(m, jnp.max(s, axis=0, keepdims=True))
            p = jnp.exp(s - m_new).astype(BF16)
            acc = acc * jnp.exp(m - m_new) + _dot(vt_ref[0, j, kb], p)
            out.append((m_new, acc))
        return tuple(out)

    init = tuple((jnp.full((1, T), NEG, F32), jnp.zeros((LANE, T), F32)) for _ in range(pair))
    carry = lax.fori_loop(0, qi, lambda kb, c: step(kb, c, False), init)
    carry = step(qi, carry, True)
    o = jnp.concatenate([acc[:hd] / acc[hd:hd + 1] for _, acc in carry], axis=0)
    o_ref[0] = (o * gt_ref[0].astype(F32)).T.astype(o_ref.dtype)


def _fox_attn(qt, k, vt, gt, hd):
    B, heads, nT = qt.shape[:3]
    D, Lp = gt.shape[1], gt.shape[2]
    pair = LANE // hd
    return pl.pallas_call(
        functools.partial(_fox_attn_kernel, hd=hd),
        grid=(B, heads // pair, nT),
        in_specs=[
            pl.BlockSpec((1, pair, 1, LANE, TOK), lambda b, p, q: (b, p, q, 0, 0)),
            pl.BlockSpec((1, pair, nT, TOK, LANE), lambda b, p, q: (b, p, 0, 0, 0)),
            pl.BlockSpec((1, pair, nT, LANE, TOK), lambda b, p, q: (b, p, 0, 0, 0)),
            pl.BlockSpec((1, LANE, TOK), lambda b, p, q: (b, p, q)),
        ],
        out_specs=pl.BlockSpec((1, TOK, LANE), lambda b, p, q: (b, q, p)),
        out_shape=jax.ShapeDtypeStruct((B, Lp, D), BF16),
        compiler_params=_cparams("arbitrary", "arbitrary", "arbitrary"),
        name="fox_attn",
    )(qt, k, vt, gt)


def _proj_res_kernel(h_ref, x_ref, w_ref, o_ref):
    o_ref[...] = h_ref[...] + _dot(x_ref[...], w_ref[...])


def _proj_res(h, x, w):
    B, Lp, D = h.shape
    rows = B * Lp
    out = pl.pallas_call(
        _proj_res_kernel,
        grid=(rows // ROW_TILE,),
        in_specs=[
            pl.BlockSpec((ROW_TILE, D), lambda i: (i, 0)),
            pl.BlockSpec((ROW_TILE, x.shape[-1]), lambda i: (i, 0)),
            _resident(w.shape),
        ],
        out_specs=pl.BlockSpec((ROW_TILE, D), lambda i: (i, 0)),
        out_shape=jax.ShapeDtypeStruct((rows, D), F32),
        compiler_params=_cparams("arbitrary"),
        name="proj_res",
    )(h.reshape(rows, D), x.reshape(rows, x.shape[-1]), w)
    return out.reshape(B, Lp, D)


def _ffn_kernel(h_ref, g_ref, wg_ref, wu_ref, cw_ref, wd_ref, o_ref, carry_ref):
    @pl.when(pl.program_id(1) == 0)
    def _():
        carry_ref[...] = jnp.zeros_like(carry_ref)

    h = h_ref[0]
    T = h.shape[0]
    F = wg_ref.shape[1]
    xn = _rms(h, g_ref[...]).astype(BF16)
    row = lax.broadcasted_iota(jnp.int32, (T, FFN_CHUNK), 0)
    acc = h
    for c in range(F // FFN_CHUNK):
        sl = slice(c * FFN_CHUNK, (c + 1) * FFN_CHUNK)
        a = _dot(xn, wg_ref[:, sl])
        u = _dot(xn, wu_ref[:, sl])
        prev = carry_ref[:, sl]
        a1 = jnp.where(row == 0, prev[SUBLANE - 1:SUBLANE], pltpu.roll(a, 1, 0))
        a2 = jnp.where(row == 0, prev[SUBLANE - 2:SUBLANE - 1],
                       jnp.where(row == 1, prev[SUBLANE - 1:SUBLANE], pltpu.roll(a, 2, 0)))
        carry_ref[:, sl] = a[T - SUBLANE:T]
        cw = cw_ref[:, sl]
        z = cw[3:4] + cw[0:1] * a2 + cw[1:2] * a1 + cw[2:3] * a
        act = (z * jax.nn.sigmoid(z) * u).astype(BF16)
        acc = acc + _dot(act, wd_ref[sl, :])
    o_ref[0] = acc


def _ffn(h, gain, w_gate, w_up, conv_w, conv_b, w_down):
    B, Lp, D = h.shape
    F = w_gate.shape[1]
    cw = jnp.concatenate([conv_w, conv_b[None], jnp.zeros((SUBLANE - conv_w.shape[0] - 1, F), F32)], axis=0)
    return pl.pallas_call(
        _ffn_kernel,
        grid=(B, Lp // TOK),
        in_specs=[
            pl.BlockSpec((1, TOK, D), lambda b, t: (b, t, 0)),
            _resident((1, D)), _resident((D, F)), _resident((D, F)),
            _resident((SUBLANE, F)), _resident((F, D)),
        ],
        out_specs=pl.BlockSpec((1, TOK, D), lambda b, t: (b, t, 0)),
        out_shape=jax.ShapeDtypeStruct((B, Lp, D), F32),
        scratch_shapes=[pltpu.VMEM((SUBLANE, F), F32)],
        compiler_params=_cparams("arbitrary", "arbitrary"),
        name="ffn",
    )(h, gain.reshape(1, D), w_gate.astype(BF16), w_up.astype(BF16), cw, w_down.astype(BF16))


def _hgrn_in_kernel(h_ref, g_ref, w_ref, lbs_ref, q_ref, lf_ref, v_ref, sg_ref, *, layer):
    D = h_ref.shape[1]
    xn = _rms(h_ref[...], g_ref[...]).astype(BF16)
    lbs = lbs_ref[...]
    e = jnp.exp(lbs - jnp.max(lbs, axis=0, keepdims=True))
    p = e / jnp.sum(e, axis=0, keepdims=True)
    lb = jnp.sum(p[1:layer + 1], axis=0, keepdims=True) if layer > 0 else jnp.zeros((1, D), F32)

    def proj(i):
        return _dot(xn, w_ref[:, i * D:(i + 1) * D])

    yq = proj(0)
    q_ref[...] = (yq * jax.nn.sigmoid(yq)).astype(q_ref.dtype)
    f = lb + (1.0 - lb) * jax.nn.sigmoid(proj(1))
    lf_ref[...] = jnp.log(f)
    v_ref[...] = proj(2).astype(v_ref.dtype)
    sg_ref[...] = jax.nn.sigmoid(proj(3)).astype(sg_ref.dtype)


def _hgrn_in(h, gain, w, lower_bounds, layer):
    B, Lp, D = h.shape
    rows = B * Lp
    blk = pl.BlockSpec((TOK, D), lambda i: (i, 0))
    outs = pl.pallas_call(
        functools.partial(_hgrn_in_kernel, layer=layer),
        grid=(rows // TOK,),
        in_specs=[blk, _resident((1, D)), _resident(w.shape), _resident(lower_bounds.shape)],
        out_specs=[blk, blk, blk, blk],
        out_shape=[
            jax.ShapeDtypeStruct((rows, D), BF16),
            jax.ShapeDtypeStruct((rows, D), F32),
            jax.ShapeDtypeStruct((rows, D), BF16),
            jax.ShapeDtypeStruct((rows, D), BF16),
        ],
        compiler_params=_cparams("arbitrary"),
        name="hgrn_in",
    )(h.reshape(rows, D), gain.reshape(1, D), w, lower_bounds)
    return [o.reshape(B, Lp, D) for o in outs]


def _hgrn_rec_kernel(q_ref, lf_ref, v_ref, sg_ref, og_ref, o_ref, st_ref, *, dk):
    @pl.when(pl.program_id(1) == 0)
    def _():
        st_ref[...] = jnp.zeros_like(st_ref)

    C, D = q_ref.shape[1], q_ref.shape[2]
    sub = HGRN_SUB
    nsub = C // sub
    r_cc = lax.broadcasted_iota(jnp.int32, (C, C), 0)
    c_cc = lax.broadcasted_iota(jnp.int32, (C, C), 1)
    ltri = jnp.where(c_cc <= r_cc, 1.0, 0.0).astype(BF16)
    sub_start = r_cc - (r_cc & (sub - 1))
    row = lax.broadcasted_iota(jnp.int32, (C, dk), 0)
    tl = lax.broadcasted_iota(jnp.int32, (nsub, sub, dk), 1)

    for hh in range(D // dk):
        sl = slice(hh * dk, (hh + 1) * dk)
        lf = lf_ref[0, :, sl]
        hi, mid, lo = _split3(lf)
        b = _dot(ltri, hi) + _dot(ltri, mid) + _dot(ltri, lo)
        q = q_ref[0, :, sl].astype(F32)
        v = v_ref[0, :, sl]
        kk = 1.0 - jnp.exp(lf)
        b_end = b[C - 1:C]
        st = st_ref[hh]

        o = _dot_nt((q * jnp.exp(b)).astype(BF16), st.astype(BF16))

        parts = [jnp.zeros((sub, C), F32)]
        for i in range(1, nsub):
            bref = b[i * sub - 1:i * sub]
            qi = q[i * sub:(i + 1) * sub] * jnp.exp(b[i * sub:(i + 1) * sub] - bref)
            ke = kk * jnp.exp(jnp.where(row < i * sub, bref - b, NEG))
            parts.append(_dot_nt(qi.astype(BF16), ke.astype(BF16)))
        a = jnp.concatenate(parts, axis=0)

        b4 = b.reshape(nsub, sub, dk)
        q4 = q.reshape(nsub, sub, dk)
        k4 = kk.reshape(nsub, sub, dk)
        for s in range(sub):
            e = jnp.exp(jnp.where(tl >= s, b4 - b4[:, s:s + 1], NEG))
            col = jnp.sum(q4 * e * k4[:, s:s + 1], axis=-1, keepdims=True).reshape(C, 1)
            a = jnp.where(c_cc == sub_start + s, col, a)

        o = o + _dot(a.astype(BF16), v)
        ke_end = (kk * jnp.exp(b_end - b)).astype(BF16)
        st_ref[hh] = st * jnp.exp(b_end) + _dot_tn(v, ke_end)

        on = _rms(o, og_ref[...]) * sg_ref[0, :, sl].astype(F32)
        o_ref[0, :, sl] = on.astype(o_ref.dtype)


def _hgrn_rec(q, lf, v, sg, o_gain):
    B, Lp, D = q.shape
    dk = o_gain.shape[0]
    blk = pl.BlockSpec((1, HGRN_CHUNK, D), lambda b, t: (b, t, 0))
    return pl.pallas_call(
        functools.partial(_hgrn_rec_kernel, dk=dk),
        grid=(B, Lp // HGRN_CHUNK),
        in_specs=[blk, blk, blk, blk, _resident((1, dk))],
        out_specs=blk,
        out_shape=jax.ShapeDtypeStruct((B, Lp, D), BF16),
        scratch_shapes=[pltpu.VMEM((D // dk, dk, dk), F32)],
        compiler_params=_cparams("arbitrary", "arbitrary"),
        name="hgrn_rec",
    )(q, lf, v, sg, o_gain.reshape(1, dk))


def kernel(x, meta_tokens, fox_norm, fox_w_in, fox_b_f, fox_q_gain, fox_k_gain, fox_w_out,
           hgrn_norm, hgrn_w_in, hgrn_lower_bounds, hgrn_o_gain, hgrn_w_out,
           ffn_norm, ffn_w_gate, ffn_w_up, ffn_conv_w, ffn_conv_b, ffn_w_down):
    B, S, D = x.shape
    L = S + N_META
    Lp = -(-L // TOK) * TOK
    depth = ffn_norm.shape[0]
    heads = fox_b_f.shape[1]
    hd = D // heads
    assert LANE % hd == 0 and hd + N_EXTRA <= LANE and D % LANE == 0
    assert (B * Lp) % ROW_TILE == 0 and Lp % HGRN_CHUNK == 0

    meta = jnp.broadcast_to(meta_tokens[None].astype(x.dtype), (B, N_META, D))
    h = jnp.concatenate([meta, x, jnp.zeros((B, Lp - L, D), x.dtype)], axis=1)

    for layer in range(depth):
        j = layer // 2
        if layer % 2 == 0:
            w = fox_w_in[j]
            w_t = jnp.concatenate([w[:, :3 * D], w[:, 3 * D + heads:]], axis=1).T.astype(BF16)
            wf_t = w[:, 3 * D:3 * D + heads].T.astype(BF16)
            qt, k, vt, gt = _fox_in(h, fox_norm[j], w_t, wf_t, fox_b_f[j],
                                    fox_q_gain[j], fox_k_gain[j], heads)
            og = _fox_attn(qt, k, vt, gt, hd)
            h = _proj_res(h, og, fox_w_out[j].astype(BF16))
        else:
            q, lf, v, sg = _hgrn_in(h, hgrn_norm[j], hgrn_w_in[j].astype(BF16),
                                    hgrn_lower_bounds, layer)
            og = _hgrn_rec(q, lf, v, sg, hgrn_o_gain[j])
            h = _proj_res(h, og, hgrn_w_out[j].astype(BF16))
        h = _ffn(h, ffn_norm[layer], ffn_w_gate[layer], ffn_w_up[layer],
                 ffn_conv_w[layer], ffn_conv_b[layer], ffn_w_down[layer])
    return h[:, N_META:L]
```

```python
import functools

import jax
import jax.numpy as jnp
from jax import lax
from jax.experimental import pallas as pl
from jax.experimental.pallas import tpu as pltpu

N_META = 16
EPS = 1e-6
NEG = -1e30
LANE = 128
SUBLANE = 8
TOK = 256
KEY_BLOCKS = 3
ROW_TILE = 512
FFN_CHUNK = 256
HGRN_CHUNK = 64
HGRN_SUB = 16
N_EXTRA = 16
VMEM_LIMIT = 56 * 1024 * 1024

F32 = jnp.float32
BF16 = jnp.bfloat16


def _cparams(*sem):
    return pltpu.CompilerParams(dimension_semantics=sem, vmem_limit_bytes=VMEM_LIMIT)


def _resident(shape):
    zeros = (0,) * len(shape)
    return pl.BlockSpec(shape, lambda *_: zeros)


def _split3(x):
    hi = x.astype(BF16)
    r = x - hi.astype(F32)
    mid = r.astype(BF16)
    lo = (r - mid.astype(F32)).astype(BF16)
    return hi, mid, lo


def _dot(a, b):
    return jnp.dot(a, b, preferred_element_type=F32)


def _dot_nt(a, b):
    return lax.dot_general(a, b, (((1,), (1,)), ((), ())), preferred_element_type=F32)


def _dot_tn(a, b):
    return lax.dot_general(a, b, (((0,), (0,)), ((), ())), preferred_element_type=F32)


def _rms(x, gain):
    ms = jnp.mean(x * x, axis=-1, keepdims=True)
    return x * lax.rsqrt(ms + EPS) * gain


def _fox_in_kernel(h_ref, g_ref, w_ref, wf_ref, bf_ref, qg_ref, kg_ref,
                   qt_ref, k_ref, vt_ref, gt_ref, carry_ref, *, heads):
    @pl.when(pl.program_id(1) == 0)
    def _():
        carry_ref[...] = jnp.zeros_like(carry_ref)

    T, D = h_ref.shape[1], h_ref.shape[2]
    hd = D // heads
    xnt = _rms(h_ref[0], g_ref[...]).T.astype(BF16)

    def proj(i):
        return _dot(w_ref[i * D:(i + 1) * D, :], xnt)

    fl = _dot(wf_ref[...], xnt) + bf_ref[...]
    lf = jnp.minimum(fl, 0.0) - jnp.log(1.0 + jnp.exp(-jnp.abs(fl)))
    src = lax.broadcasted_iota(jnp.int32, (T, T), 0)
    dst = lax.broadcasted_iota(jnp.int32, (T, T), 1)
    upper = jnp.where(src <= dst, 1.0, 0.0).astype(BF16)
    hi, mid, lo = _split3(lf)
    c = carry_ref[:, 0:1] + (_dot(hi, upper) + _dot(mid, upper) + _dot(lo, upper))
    carry_ref[...] = carry_ref[...] + jnp.sum(lf, axis=1, keepdims=True)
    chi, cmid, clo = (p.astype(F32) for p in _split3(c))

    def head_norm(y, gain_ref, scale):
        y = y.reshape(heads, hd, T)
        ms = jnp.mean(y * y, axis=1, keepdims=True)
        return y * lax.rsqrt(ms + EPS) * (gain_ref[...] * scale)

    qn = head_norm(proj(0), qg_ref, hd ** -0.5)
    kn = head_norm(proj(1), kg_ref, 1.0)
    vv = proj(2).reshape(heads, hd, T)
    gt_ref[0] = jax.nn.sigmoid(proj(3)).astype(gt_ref.dtype)

    r = lax.broadcasted_iota(jnp.int32, (N_EXTRA, T), 0)
    zpad = jnp.zeros((LANE - hd - N_EXTRA, T), F32)
    ones_row = jnp.where(r == 0, 1.0, 0.0)
    for hh in range(heads):
        a, b, d = (jnp.broadcast_to(p[hh:hh + 1], (N_EXTRA, T)) for p in (chi, cmid, clo))
        eq = jnp.where(r == 0, a, jnp.where(r == 1, b, jnp.where(r == 2, d, jnp.where(r < 6, 1.0, 0.0))))
        ek = jnp.where(r < 3, 1.0, jnp.where(r == 3, -a, jnp.where(r == 4, -b, jnp.where(r == 5, -d, 0.0))))
        qt_ref[0, hh, 0] = jnp.concatenate([qn[hh], eq, zpad], axis=0).astype(qt_ref.dtype)
        k_ref[0, hh, 0] = jnp.concatenate([kn[hh], ek, zpad], axis=0).T.astype(k_ref.dtype)
        vt_ref[0, hh, 0] = jnp.concatenate([vv[hh], ones_row, zpad], axis=0).astype(vt_ref.dtype)


def _fox_in(h, gain, w_t, wf_t, b_f, q_gain, k_gain, heads):
    B, Lp, D = h.shape
    nT = Lp // TOK
    hd = D // heads
    blk_t = pl.BlockSpec((1, heads, 1, LANE, TOK), lambda b, t: (b, 0, t, 0, 0))
    blk_vt = pl.BlockSpec((1, heads, 1, LANE, TOK), lambda b, t: (b, 0, t // KEY_BLOCKS, 0, t % KEY_BLOCKS))
    return pl.pallas_call(
        functools.partial(_fox_in_kernel, heads=heads),
        grid=(B, nT),
        in_specs=[
            pl.BlockSpec((1, TOK, D), lambda b, t: (b, t, 0)),
            _resident((1, D)), _resident(w_t.shape), _resident((heads, D)),
            _resident((heads, 1)), _resident((hd, 1)), _resident((hd, 1)),
        ],
        out_specs=[
            blk_t,
            pl.BlockSpec((1, heads, 1, TOK, LANE), lambda b, t: (b, 0, t, 0, 0)),
            blk_vt,
            pl.BlockSpec((1, D, TOK), lambda b, t: (b, 0, t)),
        ],
        out_shape=[
            jax.ShapeDtypeStruct((B, heads, nT, LANE, TOK), BF16),
            jax.ShapeDtypeStruct((B, heads, nT, TOK, LANE), BF16),
            jax.ShapeDtypeStruct((B, heads, nT // KEY_BLOCKS, LANE, KEY_BLOCKS * TOK), BF16),
            jax.ShapeDtypeStruct((B, D, Lp), BF16),
        ],
        scratch_shapes=[pltpu.VMEM((heads, LANE), F32)],
        compiler_params=_cparams("arbitrary", "arbitrary"),
        name="fox_in",
    )(h, gain.reshape(1, D), w_t, wf_t, b_f.reshape(heads, 1),
      q_gain.reshape(hd, 1), k_gain.reshape(hd, 1))


def _fox_attn_kernel(qt_ref, k_ref, vt_ref, gt_ref, o_ref, s_ref, cm_ref, m_ref, acc_ref, *, hd):
    qi = pl.program_id(2)
    T = qt_ref.shape[-1]
    pair = qt_ref.shape[1]
    CK = k_ref.shape[3]
    n_full = qi // KEY_BLOCKS

    def scores(c, slot, masked=False):
        for j in range(pair):
            s = _dot(k_ref[0, j, c], qt_ref[0, j, 0])
            if masked:
                key = c * CK - qi * T + lax.broadcasted_iota(jnp.int32, (CK, T), 0)
                qry = lax.broadcasted_iota(jnp.int32, (CK, T), 1)
                s = jnp.where(key <= qry, s, NEG)
            s_ref[slot, j] = s
            cm_ref[slot, j] = jnp.max(s, axis=0, keepdims=True)

    def accumulate(c, slot):
        for j in range(pair):
            m = m_ref[j]
            m_new = jnp.maximum(m, cm_ref[slot, j])
            p = jnp.exp(s_ref[slot, j] - m_new).astype(BF16)
            acc_ref[j] = acc_ref[j] * jnp.exp(m - m_new) + _dot(vt_ref[0, j, c], p)
            m_ref[j] = m_new

    m_ref[...] = jnp.full(m_ref.shape, NEG, F32)
    acc_ref[...] = jnp.zeros(acc_ref.shape, F32)

    steps = n_full - 1
    odd = steps % 2

    @pl.when(n_full == 0)
    def _():
        scores(0, 1, masked=True)

    @pl.when(jnp.logical_and(n_full > 0, odd == 0))
    def _():
        scores(0, 0)

    @pl.when(jnp.logical_and(n_full > 0, odd == 1))
    def _():
        scores(0, 1)
        scores(1, 0)
        accumulate(0, 1)

    def body(i, _):
        c = odd + 2 * i
        scores(c + 1, 1)
        accumulate(c, 0)
        scores(c + 2, 0)
        accumulate(c + 1, 1)
        return 0

    lax.fori_loop(0, steps // 2, body, 0)

    @pl.when(n_full > 0)
    def _():
        scores(n_full, 1, masked=True)
        accumulate(n_full - 1, 0)

    accumulate(n_full, 1)
    o = jnp.concatenate([acc_ref[j, :hd] / acc_ref[j, hd:hd + 1] for j in range(pair)], axis=0)
    o_ref[0] = (o * gt_ref[0].astype(F32)).T.astype(o_ref.dtype)


def _fox_attn(qt, k, vt, gt, hd):
    B, heads, nT = qt.shape[:3]
    nC, CK = k.shape[2], k.shape[3]
    D, Lp = gt.shape[1], gt.shape[2]
    pair = LANE // hd
    return pl.pallas_call(
        functools.partial(_fox_attn_kernel, hd=hd),
        grid=(B, heads // pair, nT),
        in_specs=[
            pl.BlockSpec((1, pair, 1, LANE, TOK), lambda b, p, q: (b, p, q, 0, 0)),
            pl.BlockSpec((1, pair, nC, CK, LANE), lambda b, p, q: (b, p, 0, 0, 0)),
            pl.BlockSpec((1, pair, nC, LANE, CK), lambda b, p, q: (b, p, 0, 0, 0)),
            pl.BlockSpec((1, LANE, TOK), lambda b, p, q: (b, p, q)),
        ],
        out_specs=pl.BlockSpec((1, TOK, LANE), lambda b, p, q: (b, q, p)),
        out_shape=jax.ShapeDtypeStruct((B, Lp, D), BF16),
        scratch_shapes=[
            pltpu.VMEM((2, pair, CK, TOK), F32),
            pltpu.VMEM((2, pair, 1, TOK), F32),
            pltpu.VMEM((pair, 1, TOK), F32),
            pltpu.VMEM((pair, LANE, TOK), F32),
        ],
        compiler_params=_cparams("arbitrary", "arbitrary", "arbitrary"),
        name="fox_attn",
    )(qt, k, vt, gt)


def _proj_res_kernel(h_ref, x_ref, w_ref, o_ref):
    o_ref[...] = h_ref[...] + _dot(x_ref[...], w_ref[...])


def _proj_res(h, x, w):
    B, Lp, D = h.shape
    rows = B * Lp
    out = pl.pallas_call(
        _proj_res_kernel,
        grid=(rows // ROW_TILE,),
        in_specs=[
            pl.BlockSpec((ROW_TILE, D), lambda i: (i, 0)),
            pl.BlockSpec((ROW_TILE, x.shape[-1]), lambda i: (i, 0)),
            _resident(w.shape),
        ],
        out_specs=pl.BlockSpec((ROW_TILE, D), lambda i: (i, 0)),
        out_shape=jax.ShapeDtypeStruct((rows, D), F32),
        compiler_params=_cparams("arbitrary"),
        name="proj_res",
    )(h.reshape(rows, D), x.reshape(rows, x.shape[-1]), w)
    return out.reshape(B, Lp, D)


def _ffn_kernel(h_ref, g_ref, wg_ref, wu_ref, cw_ref, wd_ref, o_ref, carry_ref):
    @pl.when(pl.program_id(1) == 0)
    def _():
        carry_ref[...] = jnp.zeros_like(carry_ref)

    h = h_ref[0]
    T = h.shape[0]
    F = wg_ref.shape[1]
    xn = _rms(h, g_ref[...]).astype(BF16)
    row = lax.broadcasted_iota(jnp.int32, (T, FFN_CHUNK), 0)
    acc = h
    for c in range(F // FFN_CHUNK):
        sl = slice(c * FFN_CHUNK, (c + 1) * FFN_CHUNK)
        a = _dot(xn, wg_ref[:, sl])
        u = _dot(xn, wu_ref[:, sl])
        prev = carry_ref[:, sl]
        a1 = jnp.where(row == 0, prev[SUBLANE - 1:SUBLANE], pltpu.roll(a, 1, 0))
        a2 = jnp.where(row == 0, prev[SUBLANE - 2:SUBLANE - 1],
                       jnp.where(row == 1, prev[SUBLANE - 1:SUBLANE], pltpu.roll(a, 2, 0)))
        carry_ref[:, sl] = a[T - SUBLANE:T]
        cw = cw_ref[:, sl]
        z = cw[3:4] + cw[0:1] * a2 + cw[1:2] * a1 + cw[2:3] * a
        act = (z * jax.nn.sigmoid(z) * u).astype(BF16)
        acc = acc + _dot(act, wd_ref[sl, :])
    o_ref[0] = acc


def _ffn(h, gain, w_gate, w_up, conv_w, conv_b, w_down):
    B, Lp, D = h.shape
    F = w_gate.shape[1]
    cw = jnp.concatenate([conv_w, conv_b[None], jnp.zeros((SUBLANE - conv_w.shape[0] - 1, F), F32)], axis=0)
    return pl.pallas_call(
        _ffn_kernel,
        grid=(B, Lp // TOK),
        in_specs=[
            pl.BlockSpec((1, TOK, D), lambda b, t: (b, t, 0)),
            _resident((1, D)), _resident((D, F)), _resident((D, F)),
            _resident((SUBLANE, F)), _resident((F, D)),
        ],
        out_specs=pl.BlockSpec((1, TOK, D), lambda b, t: (b, t, 0)),
        out_shape=jax.ShapeDtypeStruct((B, Lp, D), F32),
        scratch_shapes=[pltpu.VMEM((SUBLANE, F), F32)],
        compiler_params=_cparams("arbitrary", "arbitrary"),
        name="ffn",
    )(h, gain.reshape(1, D), w_gate.astype(BF16), w_up.astype(BF16), cw, w_down.astype(BF16))


def _hgrn_in_kernel(h_ref, g_ref, w_ref, lbs_ref, q_ref, lf_ref, v_ref, sg_ref, *, layer):
    D = h_ref.shape[1]
    xn = _rms(h_ref[...], g_ref[...]).astype(BF16)
    lbs = lbs_ref[...]
    e = jnp.exp(lbs - jnp.max(lbs, axis=0, keepdims=True))
    p = e / jnp.sum(e, axis=0, keepdims=True)
    lb = jnp.sum(p[1:layer + 1], axis=0, keepdims=True) if layer > 0 else jnp.zeros((1, D), F32)

    def proj(i):
        return _dot(xn, w_ref[:, i * D:(i + 1) * D])

    yq = proj(0)
    q_ref[...] = (yq * jax.nn.sigmoid(yq)).astype(q_ref.dtype)
    f = lb + (1.0 - lb) * jax.nn.sigmoid(proj(1))
    lf_ref[...] = jnp.log(f)
    v_ref[...] = proj(2).astype(v_ref.dtype)
    sg_ref[...] = jax.nn.sigmoid(proj(3)).astype(sg_ref.dtype)


def _hgrn_in(h, gain, w, lower_bounds, layer):
    B, Lp, D = h.shape
    rows = B * Lp
    blk = pl.BlockSpec((TOK, D), lambda i: (i, 0))
    outs = pl.pallas_call(
        functools.partial(_hgrn_in_kernel, layer=layer),
        grid=(rows // TOK,),
        in_specs=[blk, _resident((1, D)), _resident(w.shape), _resident(lower_bounds.shape)],
        out_specs=[blk, blk, blk, blk],
        out_shape=[
            jax.ShapeDtypeStruct((rows, D), BF16),
            jax.ShapeDtypeStruct((rows, D), F32),
            jax.ShapeDtypeStruct((rows, D), BF16),
            jax.ShapeDtypeStruct((rows, D), BF16),
        ],
        compiler_params=_cparams("arbitrary"),
        name="hgrn_in",
    )(h.reshape(rows, D), gain.reshape(1, D), w, lower_bounds)
    return [o.reshape(B, Lp, D) for o in outs]


def _hgrn_rec_kernel(q_ref, lf_ref, v_ref, sg_ref, og_ref, o_ref, st_ref, *, dk):
    @pl.when(pl.program_id(1) == 0)
    def _():
        st_ref[...] = jnp.zeros_like(st_ref)

    C, D = q_ref.shape[1], q_ref.shape[2]
    sub = HGRN_SUB
    nsub = C // sub
    r_cc = lax.broadcasted_iota(jnp.int32, (C, C), 0)
    c_cc = lax.broadcasted_iota(jnp.int32, (C, C), 1)
    ltri = jnp.where(c_cc <= r_cc, 1.0, 0.0).astype(BF16)
    sub_start = r_cc - (r_cc & (sub - 1))
    row = lax.broadcasted_iota(jnp.int32, (C, dk), 0)
    tl = lax.broadcasted_iota(jnp.int32, (nsub, sub, dk), 1)

    for hh in range(D // dk):
        sl = slice(hh * dk, (hh + 1) * dk)
        lf = lf_ref[0, :, sl]
        hi, mid, lo = _split3(lf)
        b = _dot(ltri, hi) + _dot(ltri, mid) + _dot(ltri, lo)
        q = q_ref[0, :, sl].astype(F32)
        v = v_ref[0, :, sl]
        kk = 1.0 - jnp.exp(lf)
        b_end = b[C - 1:C]
        st = st_ref[hh]

        o = _dot_nt((q * jnp.exp(b)).astype(BF16), st.astype(BF16))

        parts = [jnp.zeros((sub, C), F32)]
        for i in range(1, nsub):
            bref = b[i * sub - 1:i * sub]
            qi = q[i * sub:(i + 1) * sub] * jnp.exp(b[i * sub:(i + 1) * sub] - bref)
            ke = kk * jnp.exp(jnp.where(row < i * sub, bref - b, NEG))
            parts.append(_dot_nt(qi.astype(BF16), ke.astype(BF16)))
        a = jnp.concatenate(parts, axis=0)

        b4 = b.reshape(nsub, sub, dk)
        q4 = q.reshape(nsub, sub, dk)
        k4 = kk.reshape(nsub, sub, dk)
        for s in range(sub):
            e = jnp.exp(jnp.where(tl >= s, b4 - b4[:, s:s + 1], NEG))
            col = jnp.sum(q4 * e * k4[:, s:s + 1], axis=-1, keepdims=True).reshape(C, 1)
            a = jnp.where(c_cc == sub_start + s, col, a)

        o = o + _dot(a.astype(BF16), v)
        ke_end = (kk * jnp.exp(b_end - b)).astype(BF16)
        st_ref[hh] = st * jnp.exp(b_end) + _dot_tn(v, ke_end)

        on = _rms(o, og_ref[...]) * sg_ref[0, :, sl].astype(F32)
        o_ref[0, :, sl] = on.astype(o_ref.dtype)


def _hgrn_rec(q, lf, v, sg, o_gain):
    B, Lp, D = q.shape
    dk = o_gain.shape[0]
    blk = pl.BlockSpec((1, HGRN_CHUNK, D), lambda b, t: (b, t, 0))
    return pl.pallas_call(
        functools.partial(_hgrn_rec_kernel, dk=dk),
        grid=(B, Lp // HGRN_CHUNK),
        in_specs=[blk, blk, blk, blk, _resident((1, dk))],
        out_specs=blk,
        out_shape=jax.ShapeDtypeStruct((B, Lp, D), BF16),
        scratch_shapes=[pltpu.VMEM((D // dk, dk, dk), F32)],
        compiler_params=_cparams("arbitrary", "arbitrary"),
        name="hgrn_rec",
    )(q, lf, v, sg, o_gain.reshape(1, dk))


def kernel(x, meta_tokens, fox_norm, fox_w_in, fox_b_f, fox_q_gain, fox_k_gain, fox_w_out,
           hgrn_norm, hgrn_w_in, hgrn_lower_bounds, hgrn_o_gain, hgrn_w_out,
           ffn_norm, ffn_w_gate, ffn_w_up, ffn_conv_w, ffn_conv_b, ffn_w_down):
    B, S, D = x.shape
    L = S + N_META
    Lp = -(-L // TOK) * TOK
    depth = ffn_norm.shape[0]
    heads = fox_b_f.shape[1]
    hd = D // heads
    assert LANE % hd == 0 and hd + N_EXTRA <= LANE and D % LANE == 0
    assert (B * Lp) % ROW_TILE == 0 and Lp % HGRN_CHUNK == 0 and (Lp // TOK) % KEY_BLOCKS == 0

    meta = jnp.broadcast_to(meta_tokens[None].astype(x.dtype), (B, N_META, D))
    h = jnp.concatenate([meta, x, jnp.zeros((B, Lp - L, D), x.dtype)], axis=1)

    for layer in range(depth):
        j = layer // 2
        if layer % 2 == 0:
            w = fox_w_in[j]
            w_t = jnp.concatenate([w[:, :3 * D], w[:, 3 * D + heads:]], axis=1).T.astype(BF16)
            wf_t = w[:, 3 * D:3 * D + heads].T.astype(BF16)
            qt, k, vt, gt = _fox_in(h, fox_norm[j], w_t, wf_t, fox_b_f[j],
                                    fox_q_gain[j], fox_k_gain[j], heads)
            k = k.reshape(B, heads, -1, KEY_BLOCKS * TOK, LANE)
            og = _fox_attn(qt, k, vt, gt, hd)
            h = _proj_res(h, og, fox_w_out[j].astype(BF16))
        else:
            q, lf, v, sg = _hgrn_in(h, hgrn_norm[j], hgrn_w_in[j].astype(BF16),
                                    hgrn_lower_bounds, layer)
            og = _hgrn_rec(q, lf, v, sg, hgrn_o_gain[j])
            h = _proj_res(h, og, hgrn_w_out[j].astype(BF16))
        h = _ffn(h, ffn_norm[layer], ffn_w_gate[layer], ffn_w_up[layer],
                 ffn_conv_w[layer], ffn_conv_b[layer], ffn_w_down[layer])
    return h[:, N_META:L]
```

```python
import functools

import jax
import jax.numpy as jnp
from jax import lax
from jax.experimental import pallas as pl
from jax.experimental.pallas import tpu as pltpu

N_META = 16
EPS = 1e-6
NEG = -1e30
LANE = 128
SUBLANE = 8
TOK = 256
KEY_BLOCKS = 3
Q_BLOCKS = 3
ROW_TILE = 512
FFN_ROWS = 768
FFN_CHUNK = 256
HGRN_CHUNK = 64
HGRN_SUB = 16
HGRN_ROWS = 128
HGRN_SAFE_DECAY = 60.0
LOG2E = 1.4426950408889634
N_EXTRA = 16
VMEM_LIMIT = 56 * 1024 * 1024

F32 = jnp.float32
BF16 = jnp.bfloat16


def _cparams(*sem):
    return pltpu.CompilerParams(dimension_semantics=sem, vmem_limit_bytes=VMEM_LIMIT)


def _resident(shape):
    zeros = (0,) * len(shape)
    return pl.BlockSpec(shape, lambda *_: zeros, pipeline_mode=pl.Buffered(1))


def _split3(x):
    hi = x.astype(BF16)
    r = x - hi.astype(F32)
    mid = r.astype(BF16)
    lo = (r - mid.astype(F32)).astype(BF16)
    return hi, mid, lo


def _dot(a, b):
    return jnp.dot(a, b, preferred_element_type=F32)


def _dot_nt(a, b):
    return lax.dot_general(a, b, (((1,), (1,)), ((), ())), preferred_element_type=F32)


def _dot_tn(a, b):
    return lax.dot_general(a, b, (((0,), (0,)), ((), ())), preferred_element_type=F32)


def _rms(x, gain):
    ms = jnp.mean(x * x, axis=-1, keepdims=True)
    return x * lax.rsqrt(ms + EPS) * gain


def _fox_in_kernel(h_ref, g_ref, w_ref, wf_ref, bf_ref, qg_ref, kg_ref,
                   qt_ref, k_ref, vt_ref, gt_ref, carry_ref, *, heads):
    @pl.when(pl.program_id(1) == 0)
    def _():
        carry_ref[...] = jnp.zeros_like(carry_ref)

    T, D = h_ref.shape[1], h_ref.shape[2]
    hd = D // heads
    xnt = _rms(h_ref[0], g_ref[...]).T.astype(BF16)

    def proj(i):
        return _dot(w_ref[i * D:(i + 1) * D, :], xnt)

    fl = _dot(wf_ref[...], xnt) + bf_ref[...]
    lf = jnp.minimum(fl, 0.0) - jnp.log(1.0 + jnp.exp(-jnp.abs(fl)))
    src = lax.broadcasted_iota(jnp.int32, (T, T), 0)
    dst = lax.broadcasted_iota(jnp.int32, (T, T), 1)
    upper = jnp.where(src <= dst, 1.0, 0.0).astype(BF16)
    hi, mid, lo = _split3(lf)
    c = carry_ref[:, 0:1] + (_dot(hi, upper) + _dot(mid, upper) + _dot(lo, upper))
    carry_ref[...] = carry_ref[...] + jnp.sum(lf, axis=1, keepdims=True)
    chi, cmid, clo = (p.astype(F32) for p in _split3(c))

    def head_norm(y, gain_ref, scale):
        y = y.reshape(heads, hd, T)
        ms = jnp.mean(y * y, axis=1, keepdims=True)
        return y * lax.rsqrt(ms + EPS) * (gain_ref[...] * scale)

    qn = head_norm(proj(0), qg_ref, hd ** -0.5)
    kn = head_norm(proj(1), kg_ref, 1.0)
    vv = proj(2).reshape(heads, hd, T)
    gt_ref[0] = jax.nn.sigmoid(proj(3)).astype(gt_ref.dtype)

    r = lax.broadcasted_iota(jnp.int32, (N_EXTRA, T), 0)
    zpad = jnp.zeros((LANE - hd - N_EXTRA, T), F32)
    ones_row = jnp.where(r == 0, 1.0, 0.0)
    for hh in range(heads):
        a, b, d = (jnp.broadcast_to(p[hh:hh + 1], (N_EXTRA, T)) for p in (chi, cmid, clo))
        eq = jnp.where(r == 0, a, jnp.where(r == 1, b, jnp.where(r == 2, d, jnp.where(r < 6, 1.0, 0.0))))
        ek = jnp.where(r < 3, 1.0, jnp.where(r == 3, -a, jnp.where(r == 4, -b, jnp.where(r == 5, -d, 0.0))))
        qt_ref[0, hh, 0] = jnp.concatenate([qn[hh], eq, zpad], axis=0).astype(qt_ref.dtype)
        k_ref[0, hh, 0] = jnp.concatenate([kn[hh], ek, zpad], axis=0).T.astype(k_ref.dtype)
        vt_ref[0, hh, 0] = jnp.concatenate([vv[hh], ones_row, zpad], axis=0).astype(vt_ref.dtype)


def _fox_in(h, gain, w_t, wf_t, b_f, q_gain, k_gain, heads):
    B, Lp, D = h.shape
    nT = Lp // TOK
    hd = D // heads
    blk_qt = pl.BlockSpec((1, heads, 1, LANE, TOK), lambda b, t: (b, 0, t // Q_BLOCKS, 0, t % Q_BLOCKS))
    blk_vt = pl.BlockSpec((1, heads, 1, LANE, TOK), lambda b, t: (b, 0, t // KEY_BLOCKS, 0, t % KEY_BLOCKS))
    return pl.pallas_call(
        functools.partial(_fox_in_kernel, heads=heads),
        grid=(B, nT),
        in_specs=[
            pl.BlockSpec((1, TOK, D), lambda b, t: (b, t, 0)),
            _resident((1, D)), _resident(w_t.shape), _resident((heads, D)),
            _resident((heads, 1)), _resident((hd, 1)), _resident((hd, 1)),
        ],
        out_specs=[
            blk_qt,
            pl.BlockSpec((1, heads, 1, TOK, LANE), lambda b, t: (b, 0, t, 0, 0)),
            blk_vt,
            pl.BlockSpec((1, D, TOK), lambda b, t: (b, 0, t)),
        ],
        out_shape=[
            jax.ShapeDtypeStruct((B, heads, nT // Q_BLOCKS, LANE, Q_BLOCKS * TOK), BF16),
            jax.ShapeDtypeStruct((B, heads, nT, TOK, LANE), BF16),
            jax.ShapeDtypeStruct((B, heads, nT // KEY_BLOCKS, LANE, KEY_BLOCKS * TOK), BF16),
            jax.ShapeDtypeStruct((B, D, Lp), BF16),
        ],
        scratch_shapes=[pltpu.VMEM((heads, LANE), F32)],
        compiler_params=_cparams("arbitrary", "arbitrary"),
        name="fox_in",
    )(h, gain.reshape(1, D), w_t, wf_t, b_f.reshape(heads, 1),
      q_gain.reshape(hd, 1), k_gain.reshape(hd, 1))


def _fox_attn_kernel(qt_ref, k_ref, vt_ref, gt_ref, o_ref, s_ref, cm_ref, m_ref, acc_ref, *, hd):
    qi = pl.program_id(2)
    T = qt_ref.shape[-1]
    pair = qt_ref.shape[1]
    CK = k_ref.shape[3]
    n_full = (qi * T) // CK

    def scores(c, slot, masked=False):
        for j in range(pair):
            s = _dot(k_ref[0, j, c], qt_ref[0, j, 0])
            if masked:
                key = c * CK - qi * T + lax.broadcasted_iota(jnp.int32, (CK, T), 0)
                qry = lax.broadcasted_iota(jnp.int32, (CK, T), 1)
                s = jnp.where(key <= qry, s, NEG)
            s_ref[slot, j] = s
            cm_ref[slot, j] = jnp.max(s, axis=0, keepdims=True)

    def accumulate(c, slot):
        for j in range(pair):
            m = m_ref[j]
            m_new = jnp.maximum(m, cm_ref[slot, j])
            p = jnp.exp(s_ref[slot, j] - m_new).astype(BF16)
            acc_ref[j] = acc_ref[j] * jnp.exp(m - m_new) + _dot(vt_ref[0, j, c], p)
            m_ref[j] = m_new

    m_ref[...] = jnp.full(m_ref.shape, NEG, F32)
    acc_ref[...] = jnp.zeros(acc_ref.shape, F32)

    steps = n_full - 1
    odd = steps % 2

    @pl.when(n_full == 0)
    def _():
        scores(0, 1, masked=True)

    @pl.when(jnp.logical_and(n_full > 0, odd == 0))
    def _():
        scores(0, 0)

    @pl.when(jnp.logical_and(n_full > 0, odd == 1))
    def _():
        scores(0, 1)
        scores(1, 0)
        accumulate(0, 1)

    def body(i, _):
        c = odd + 2 * i
        scores(c + 1, 1)
        accumulate(c, 0)
        scores(c + 2, 0)
        accumulate(c + 1, 1)
        return 0

    lax.fori_loop(0, steps // 2, body, 0)

    @pl.when(n_full > 0)
    def _():
        scores(n_full, 1, masked=True)
        accumulate(n_full - 1, 0)

    accumulate(n_full, 1)
    o = jnp.concatenate([acc_ref[j, :hd] / acc_ref[j, hd:hd + 1] for j in range(pair)], axis=0)
    o_ref[0] = (o * gt_ref[0].astype(F32)).T.astype(o_ref.dtype)


def _fox_attn(qt, k, vt, gt, hd):
    B, heads, nQ = qt.shape[:3]
    TQ = qt.shape[-1]
    nC, CK = k.shape[2], k.shape[3]
    D, Lp = gt.shape[1], gt.shape[2]
    pair = LANE // hd
    return pl.pallas_call(
        functools.partial(_fox_attn_kernel, hd=hd),
        grid=(B, heads // pair, nQ),
        in_specs=[
            pl.BlockSpec((1, pair, 1, LANE, TQ), lambda b, p, q: (b, p, q, 0, 0)),
            pl.BlockSpec((1, pair, nC, CK, LANE), lambda b, p, q: (b, p, 0, 0, 0)),
            pl.BlockSpec((1, pair, nC, LANE, CK), lambda b, p, q: (b, p, 0, 0, 0)),
            pl.BlockSpec((1, LANE, TQ), lambda b, p, q: (b, p, q)),
        ],
        out_specs=pl.BlockSpec((1, TQ, LANE), lambda b, p, q: (b, q, p)),
        out_shape=jax.ShapeDtypeStruct((B, Lp, D), BF16),
        scratch_shapes=[
            pltpu.VMEM((2, pair, CK, TQ), F32),
            pltpu.VMEM((2, pair, 1, TQ), F32),
            pltpu.VMEM((pair, 1, TQ), F32),
            pltpu.VMEM((pair, LANE, TQ), F32),
        ],
        compiler_params=_cparams("arbitrary", "arbitrary", "arbitrary"),
        name="fox_attn",
    )(qt, k, vt, gt)


def _proj_res_kernel(h_ref, x_ref, w_ref, o_ref):
    o_ref[...] = h_ref[...] + _dot(x_ref[...], w_ref[...])


def _proj_res(h, x, w):
    B, Lp, D = h.shape
    rows = B * Lp
    out = pl.pallas_call(
        _proj_res_kernel,
        grid=(rows // ROW_TILE,),
        in_specs=[
            pl.BlockSpec((ROW_TILE, D), lambda i: (i, 0)),
            pl.BlockSpec((ROW_TILE, x.shape[-1]), lambda i: (i, 0)),
            _resident(w.shape),
        ],
        out_specs=pl.BlockSpec((ROW_TILE, D), lambda i: (i, 0)),
        out_shape=jax.ShapeDtypeStruct((rows, D), F32),
        compiler_params=_cparams("arbitrary"),
        name="proj_res",
    )(h.reshape(rows, D), x.reshape(rows, x.shape[-1]), w)
    return out.reshape(B, Lp, D)


def _ffn_kernel(h_ref, g_ref, wg_ref, wu_ref, cw_ref, wd_ref, o_ref, carry_ref):
    @pl.when(pl.program_id(1) == 0)
    def _():
        carry_ref[...] = jnp.zeros_like(carry_ref)

    h = h_ref[0]
    T = h.shape[0]
    F = wg_ref.shape[1]
    xn = _rms(h, g_ref[...]).astype(BF16)
    row = lax.broadcasted_iota(jnp.int32, (T, FFN_CHUNK), 0)
    n_chunks = F // FFN_CHUNK

    def gate_up(c):
        sl = slice(c * FFN_CHUNK, (c + 1) * FFN_CHUNK)
        return _dot(xn, wg_ref[:, sl]), _dot(xn, wu_ref[:, sl])

    acc = h
    nxt = gate_up(0)
    for c in range(n_chunks):
        sl = slice(c * FFN_CHUNK, (c + 1) * FFN_CHUNK)
        a, u = nxt
        if c + 1 < n_chunks:
            nxt = gate_up(c + 1)
        prev = carry_ref[:, sl]
        a1 = jnp.where(row == 0, prev[SUBLANE - 1:SUBLANE], pltpu.roll(a, 1, 0))
        a2 = jnp.where(row == 0, prev[SUBLANE - 2:SUBLANE - 1],
                       jnp.where(row == 1, prev[SUBLANE - 1:SUBLANE], pltpu.roll(a, 2, 0)))
        carry_ref[:, sl] = a[T - SUBLANE:T]
        cw = cw_ref[:, sl]
        z = cw[3:4] + cw[0:1] * a2 + cw[1:2] * a1 + cw[2:3] * a
        act = (z * jax.nn.sigmoid(z) * u).astype(BF16)
        acc = acc + _dot(act, wd_ref[sl, :])
    o_ref[0] = acc


def _ffn(h, gain, w_gate, w_up, conv_w, conv_b, w_down):
    B, Lp, D = h.shape
    F = w_gate.shape[1]
    cw = jnp.concatenate([conv_w, conv_b[None], jnp.zeros((SUBLANE - conv_w.shape[0] - 1, F), F32)], axis=0)
    return pl.pallas_call(
        _ffn_kernel,
        grid=(B, Lp // FFN_ROWS),
        in_specs=[
            pl.BlockSpec((1, FFN_ROWS, D), lambda b, t: (b, t, 0)),
            _resident((1, D)), _resident((D, F)), _resident((D, F)),
            _resident((SUBLANE, F)), _resident((F, D)),
        ],
        out_specs=pl.BlockSpec((1, FFN_ROWS, D), lambda b, t: (b, t, 0)),
        out_shape=jax.ShapeDtypeStruct((B, Lp, D), F32),
        scratch_shapes=[pltpu.VMEM((SUBLANE, F), F32)],
        compiler_params=_cparams("arbitrary", "arbitrary"),
        name="ffn",
    )(h, gain.reshape(1, D), w_gate.astype(BF16), w_up.astype(BF16), cw, w_down.astype(BF16))


def _hgrn_in_kernel(h_ref, g_ref, w_ref, lbs_ref, q_ref, lf_ref, v_ref, sg_ref, *, layer):
    D = h_ref.shape[1]
    xn = _rms(h_ref[...], g_ref[...]).astype(BF16)
    lbs = lbs_ref[...]
    e = jnp.exp(lbs - jnp.max(lbs, axis=0, keepdims=True))
    p = e / jnp.sum(e, axis=0, keepdims=True)
    lb = jnp.sum(p[1:layer + 1], axis=0, keepdims=True) if layer > 0 else jnp.zeros((1, D), F32)

    def proj(i):
        return _dot(xn, w_ref[:, i * D:(i + 1) * D])

    yq = proj(0)
    q_ref[...] = (yq * jax.nn.sigmoid(yq)).astype(q_ref.dtype)
    f = lb + (1.0 - lb) * jax.nn.sigmoid(proj(1))
    lf_ref[...] = jnp.log(f)
    v_ref[...] = proj(2).astype(v_ref.dtype)
    sg_ref[...] = jax.nn.sigmoid(proj(3)).astype(sg_ref.dtype)


def _hgrn_in(h, gain, w, lower_bounds, layer):
    B, Lp, D = h.shape
    rows = B * Lp
    blk = pl.BlockSpec((TOK, D), lambda i: (i, 0))
    outs = pl.pallas_call(
        functools.partial(_hgrn_in_kernel, layer=layer),
        grid=(rows // TOK,),
        in_specs=[blk, _resident((1, D)), _resident(w.shape), _resident(lower_bounds.shape)],
        out_specs=[blk, blk, blk, blk],
        out_shape=[
            jax.ShapeDtypeStruct((rows, D), BF16),
            jax.ShapeDtypeStruct((rows, D), F32),
            jax.ShapeDtypeStruct((rows, D), BF16),
            jax.ShapeDtypeStruct((rows, D), BF16),
        ],
        compiler_params=_cparams("arbitrary"),
        name="hgrn_in",
    )(h.reshape(rows, D), gain.reshape(1, D), w, lower_bounds)
    return [o.reshape(B, Lp, D) for o in outs]


def _hgrn_rec_kernel(q_ref, lf_ref, v_ref, sg_ref, og_ref, o_ref, st_ref, *, dk):
    @pl.when(pl.program_id(1) == 0)
    def _():
        st_ref[...] = jnp.zeros_like(st_ref)

    C = HGRN_CHUNK
    R, D = q_ref.shape[1], q_ref.shape[2]
    sub = HGRN_SUB
    nsub = C // sub
    r_cc = lax.broadcasted_iota(jnp.int32, (C, C), 0)
    c_cc = lax.broadcasted_iota(jnp.int32, (C, C), 1)
    causal = c_cc <= r_cc
    ltri = jnp.where(causal, 1.0, 0.0).astype(BF16)
    row = lax.broadcasted_iota(jnp.int32, (C, dk), 0)

    def run(factored_diag):
        if not factored_diag:
            diag_mask = jnp.logical_and(c_cc >= r_cc - (r_cc & (sub - 1)), causal)
            d_sel = (lax.broadcasted_iota(jnp.int32, (sub * dk, C), 0)
                     - (lax.broadcasted_iota(jnp.int32, (sub * dk, C), 1) & (sub - 1)) * dk)
            sel = jnp.where(jnp.logical_and(d_sel >= 0, d_sel < dk), 1.0, 0.0).astype(BF16)
        n_keys = [(i + 1) * sub if factored_diag else i * sub for i in range(nsub)]

        units = [(hh, ci) for hh in range(D // dk) for ci in range(R // C)]

        def blk(ref, u):
            hh, ci = u
            return ref[0, ci * C:(ci + 1) * C, hh * dk:(hh + 1) * dk]

        b = {}
        for u in units:
            hi, mid, lo = _split3(blk(lf_ref, u))
            b[u] = (_dot(ltri, hi) + _dot(ltri, mid) + _dot(ltri, lo)) * LOG2E

        qe, kk, strips = {}, {}, {}
        for u in units:
            q = blk(q_ref, u).astype(F32)
            kk[u] = 1.0 - jnp.exp(blk(lf_ref, u))
            qe[u] = (q * jnp.exp2(b[u])).astype(BF16)
            parts = []
            for i in range(nsub):
                if n_keys[i] == 0:
                    parts.append(jnp.zeros((sub, C), F32))
                    continue
                bref = b[u][i * sub - 1:i * sub] if i > 0 else jnp.zeros((1, dk), F32)
                qi = q[i * sub:(i + 1) * sub] * jnp.exp2(b[u][i * sub:(i + 1) * sub] - bref)
                ke = kk[u] * jnp.exp2(jnp.where(row < n_keys[i], bref - b[u], NEG))
                parts.append(_dot_nt(qi.astype(BF16), ke.astype(BF16)))
            strips[u] = parts

        intra, kv = {}, {}
        for u in units:
            a = jnp.concatenate(strips[u], axis=0)
            if factored_diag:
                a = jnp.where(causal, a, 0.0)
            else:
                b4 = b[u].reshape(nsub, sub, dk)
                q4 = blk(q_ref, u).astype(F32).reshape(nsub, sub, dk)
                k4 = kk[u].reshape(nsub, sub, dk)
                z = [(q4 * jnp.exp2(jnp.minimum(b4 - b4[:, s:s + 1], 0.0)) * k4[:, s:s + 1])
                     .reshape(C, dk).astype(BF16) for s in range(sub)]
                a = jnp.where(diag_mask, _dot(jnp.concatenate(z, axis=1), sel), a)
            v = blk(v_ref, u)
            intra[u] = _dot(a.astype(BF16), v)
            kv[u] = _dot_tn(v, (kk[u] * jnp.exp2(b[u][C - 1:C] - b[u])).astype(BF16))

        out = {}
        for hh in range(D // dk):
            st = st_ref[hh]
            for ci in range(R // C):
                u = (hh, ci)
                out[u] = _dot_nt(qe[u], st.astype(BF16)) + intra[u]
                st = st * jnp.exp2(b[u][C - 1:C]) + kv[u]
            st_ref[hh] = st

        for u in units:
            hh, ci = u
            on = _rms(out[u], og_ref[...]) * blk(sg_ref, u).astype(F32)
            o_ref[0, ci * C:(ci + 1) * C, hh * dk:(hh + 1) * dk] = on.astype(o_ref.dtype)

    worst = -jnp.min(jnp.sum(lf_ref[0].reshape(R // sub, sub, D), axis=1))
    safe = worst <= HGRN_SAFE_DECAY

    @pl.when(safe)
    def _():
        run(True)

    @pl.when(jnp.logical_not(safe))
    def _():
        run(False)


def _hgrn_rec(q, lf, v, sg, o_gain):
    B, Lp, D = q.shape
    dk = o_gain.shape[0]
    blk = pl.BlockSpec((1, HGRN_ROWS, D), lambda b, t: (b, t, 0))
    return pl.pallas_call(
        functools.partial(_hgrn_rec_kernel, dk=dk),
        grid=(B, Lp // HGRN_ROWS),
        in_specs=[blk, blk, blk, blk, _resident((1, dk))],
        out_specs=blk,
        out_shape=jax.ShapeDtypeStruct((B, Lp, D), BF16),
        scratch_shapes=[pltpu.VMEM((D // dk, dk, dk), F32)],
        compiler_params=_cparams("arbitrary", "arbitrary"),
        name="hgrn_rec",
    )(q, lf, v, sg, o_gain.reshape(1, dk))


def kernel(x, meta_tokens, fox_norm, fox_w_in, fox_b_f, fox_q_gain, fox_k_gain, fox_w_out,
           hgrn_norm, hgrn_w_in, hgrn_lower_bounds, hgrn_o_gain, hgrn_w_out,
           ffn_norm, ffn_w_gate, ffn_w_up, ffn_conv_w, ffn_conv_b, ffn_w_down):
    B, S, D = x.shape
    L = S + N_META
    Lp = -(-L // TOK) * TOK
    depth = ffn_norm.shape[0]
    heads = fox_b_f.shape[1]
    hd = D // heads
    assert LANE % hd == 0 and hd + N_EXTRA <= LANE and D % LANE == 0
    assert (B * Lp) % ROW_TILE == 0 and Lp % HGRN_ROWS == 0 and HGRN_ROWS % HGRN_CHUNK == 0 and (Lp // TOK) % KEY_BLOCKS == 0 and (Lp // TOK) % Q_BLOCKS == 0 and KEY_BLOCKS % Q_BLOCKS == 0

    meta = jnp.broadcast_to(meta_tokens[None].astype(x.dtype), (B, N_META, D))
    h = jnp.concatenate([meta, x, jnp.zeros((B, Lp - L, D), x.dtype)], axis=1)

    for layer in range(depth):
        j = layer // 2
        if layer % 2 == 0:
            w = fox_w_in[j]
            w_t = jnp.concatenate([w[:, :3 * D], w[:, 3 * D + heads:]], axis=1).T.astype(BF16)
            wf_t = w[:, 3 * D:3 * D + heads].T.astype(BF16)
            qt, k, vt, gt = _fox_in(h, fox_norm[j], w_t, wf_t, fox_b_f[j],
                                    fox_q_gain[j], fox_k_gain[j], heads)
            k = k.reshape(B, heads, -1, KEY_BLOCKS * TOK, LANE)
            og = _fox_attn(qt, k, vt, gt, hd)
            h = _proj_res(h, og, fox_w_out[j].astype(BF16))
        else:
            q, lf, v, sg = _hgrn_in(h, hgrn_norm[j], hgrn_w_in[j].astype(BF16),
                                    hgrn_lower_bounds, layer)
            og = _hgrn_rec(q, lf, v, sg, hgrn_o_gain[j])
            h = _proj_res(h, og, hgrn_w_out[j].astype(BF16))
        h = _ffn(h, ffn_norm[layer], ffn_w_gate[layer], ffn_w_up[layer],
                 ffn_conv_w[layer], ffn_conv_b[layer], ffn_w_down[layer])
    return h[:, N_META:L]
```

```python
import functools

import jax
import jax.numpy as jnp
from jax import lax
from jax.experimental import pallas as pl
from jax.experimental.pallas import tpu as pltpu

N_META = 16
EPS = 1e-6
NEG = -1e30
LANE = 128
SUBLANE = 8
TOK = 256
KEY_BLOCKS = 3
Q_BLOCKS = 3
FFN_ROWS = 768
FFN_SUB = 256
FFN_CHUNK = 256
HGRN_CHUNK = 64
HGRN_SUB = 16
HGRN_ROWS = 128
HGRN_SAFE_DECAY = 60.0
LOG2E = 1.4426950408889634
N_EXTRA = 16
VMEM_LIMIT = 56 * 1024 * 1024

F32 = jnp.float32
BF16 = jnp.bfloat16


def _cparams(*sem):
    return pltpu.CompilerParams(dimension_semantics=sem, vmem_limit_bytes=VMEM_LIMIT)


def _resident(shape):
    zeros = (0,) * len(shape)
    return pl.BlockSpec(shape, lambda *_: zeros, pipeline_mode=pl.Buffered(1))


def _split3(x):
    hi = x.astype(BF16)
    r = x - hi.astype(F32)
    mid = r.astype(BF16)
    lo = (r - mid.astype(F32)).astype(BF16)
    return hi, mid, lo


def _dot(a, b):
    return jnp.dot(a, b, preferred_element_type=F32)


def _dot_nt(a, b):
    return lax.dot_general(a, b, (((1,), (1,)), ((), ())), preferred_element_type=F32)


def _dot_tn(a, b):
    return lax.dot_general(a, b, (((0,), (0,)), ((), ())), preferred_element_type=F32)


def _rms(x, gain):
    ms = jnp.mean(x * x, axis=-1, keepdims=True)
    return x * lax.rsqrt(ms + EPS) * gain


def _fox_in_kernel(h_ref, g_ref, w_ref, wf_ref, bf_ref, qg_ref, kg_ref,
                   qt_ref, k_ref, vt_ref, gt_ref, carry_ref, *, heads):
    @pl.when(pl.program_id(1) == 0)
    def _():
        carry_ref[...] = jnp.zeros_like(carry_ref)

    T, D = h_ref.shape[1], h_ref.shape[2]
    hd = D // heads
    xnt = _rms(h_ref[0], g_ref[...]).T.astype(BF16)

    def proj(i):
        return _dot(w_ref[i * D:(i + 1) * D, :], xnt)

    fl = _dot(wf_ref[...], xnt) + bf_ref[...]
    lf = jnp.minimum(fl, 0.0) - jnp.log(1.0 + jnp.exp(-jnp.abs(fl)))
    src = lax.broadcasted_iota(jnp.int32, (T, T), 0)
    dst = lax.broadcasted_iota(jnp.int32, (T, T), 1)
    upper = jnp.where(src <= dst, 1.0, 0.0).astype(BF16)
    hi, mid, lo = _split3(lf)
    c = carry_ref[:, 0:1] + (_dot(hi, upper) + _dot(mid, upper) + _dot(lo, upper))
    carry_ref[...] = carry_ref[...] + jnp.sum(lf, axis=1, keepdims=True)
    chi, cmid, clo = (p.astype(F32) for p in _split3(c * LOG2E))

    def head_norm(y, gain_ref, scale):
        y = y.reshape(heads, hd, T)
        ms = jnp.mean(y * y, axis=1, keepdims=True)
        return y * lax.rsqrt(ms + EPS) * (gain_ref[...] * scale)

    qn = head_norm(proj(0), qg_ref, hd ** -0.5 * LOG2E)
    kn = head_norm(proj(1), kg_ref, 1.0)
    vv = proj(2).reshape(heads, hd, T)
    gt_ref[0] = jax.nn.sigmoid(proj(3)).astype(gt_ref.dtype)

    r = lax.broadcasted_iota(jnp.int32, (N_EXTRA, T), 0)
    zpad = jnp.zeros((LANE - hd - N_EXTRA, T), F32)
    ones_row = jnp.where(r == 0, 1.0, 0.0)
    for hh in range(heads):
        a, b, d = (jnp.broadcast_to(p[hh:hh + 1], (N_EXTRA, T)) for p in (chi, cmid, clo))
        eq = jnp.where(r == 0, a, jnp.where(r == 1, b, jnp.where(r == 2, d, jnp.where(r < 6, 1.0, 0.0))))
        ek = jnp.where(r < 3, 1.0, jnp.where(r == 3, -a, jnp.where(r == 4, -b, jnp.where(r == 5, -d, 0.0))))
        qt_ref[0, hh, 0] = jnp.concatenate([qn[hh], eq, zpad], axis=0).astype(qt_ref.dtype)
        k_ref[0, hh, 0] = jnp.concatenate([kn[hh], ek, zpad], axis=0).T.astype(k_ref.dtype)
        vt_ref[0, hh, 0] = jnp.concatenate([vv[hh], ones_row], axis=0).astype(vt_ref.dtype)


def _fox_in(h, gain, w_t, wf_t, b_f, q_gain, k_gain, heads):
    B, Lp, D = h.shape
    nT = Lp // TOK
    hd = D // heads
    blk_qt = pl.BlockSpec((1, heads, 1, LANE, TOK), lambda b, t: (b, 0, t // Q_BLOCKS, 0, t % Q_BLOCKS))
    blk_vt = pl.BlockSpec((1, heads, 1, hd + N_EXTRA, TOK),
                          lambda b, t: (b, 0, t // KEY_BLOCKS, 0, t % KEY_BLOCKS))
    return pl.pallas_call(
        functools.partial(_fox_in_kernel, heads=heads),
        grid=(B, nT),
        in_specs=[
            pl.BlockSpec((1, TOK, D), lambda b, t: (b, t, 0)),
            _resident((1, D)), _resident(w_t.shape), _resident((heads, D)),
            _resident((heads, 1)), _resident((hd, 1)), _resident((hd, 1)),
        ],
        out_specs=[
            blk_qt,
            pl.BlockSpec((1, heads, 1, TOK, LANE), lambda b, t: (b, 0, t, 0, 0)),
            blk_vt,
            pl.BlockSpec((1, D, TOK), lambda b, t: (b, 0, t)),
        ],
        out_shape=[
            jax.ShapeDtypeStruct((B, heads, nT // Q_BLOCKS, LANE, Q_BLOCKS * TOK), BF16),
            jax.ShapeDtypeStruct((B, heads, nT, TOK, LANE), BF16),
            jax.ShapeDtypeStruct((B, heads, nT // KEY_BLOCKS, hd + N_EXTRA, KEY_BLOCKS * TOK), BF16),
            jax.ShapeDtypeStruct((B, D, Lp), BF16),
        ],
        scratch_shapes=[pltpu.VMEM((heads, LANE), F32)],
        compiler_params=_cparams("arbitrary", "arbitrary"),
        name="fox_in",
    )(h, gain.reshape(1, D), w_t, wf_t, b_f.reshape(heads, 1),
      q_gain.reshape(hd, 1), k_gain.reshape(hd, 1))


def _fox_attn_kernel(qt_ref, k_ref, vt_ref, gt_ref, o_ref, s_ref, cm_ref, m_ref, acc_ref, *, hd):
    qi = pl.program_id(2)
    T = qt_ref.shape[-1]
    pair = qt_ref.shape[1]
    CK = k_ref.shape[3]
    n_full = (qi * T) // CK

    def scores(c, slot, masked=False):
        for j in range(pair):
            s = _dot(k_ref[0, j, c], qt_ref[0, j, 0])
            if masked:
                key = c * CK - qi * T + lax.broadcasted_iota(jnp.int32, (CK, T), 0)
                qry = lax.broadcasted_iota(jnp.int32, (CK, T), 1)
                s = jnp.where(key <= qry, s, NEG)
            s_ref[slot, j] = s
            cm_ref[slot, j] = jnp.max(s, axis=0, keepdims=True)

    def accumulate(c, slot):
        for j in range(pair):
            m = m_ref[j]
            m_new = jnp.maximum(m, cm_ref[slot, j])
            p = jnp.exp2(s_ref[slot, j] - m_new).astype(BF16)
            acc_ref[j] = acc_ref[j] * jnp.exp2(m - m_new) + _dot(vt_ref[0, j, c], p)
            m_ref[j] = m_new

    m_ref[...] = jnp.full(m_ref.shape, NEG, F32)
    acc_ref[...] = jnp.zeros(acc_ref.shape, F32)

    steps = n_full - 1
    odd = steps % 2

    @pl.when(n_full == 0)
    def _():
        scores(0, 1, masked=True)

    @pl.when(jnp.logical_and(n_full > 0, odd == 0))
    def _():
        scores(0, 0)

    @pl.when(jnp.logical_and(n_full > 0, odd == 1))
    def _():
        scores(0, 1)
        scores(1, 0)
        accumulate(0, 1)

    def body(i, _):
        c = odd + 2 * i
        scores(c + 1, 1)
        accumulate(c, 0)
        scores(c + 2, 0)
        accumulate(c + 1, 1)
        return 0

    lax.fori_loop(0, steps // 2, body, 0)

    @pl.when(n_full > 0)
    def _():
        scores(n_full, 1, masked=True)
        accumulate(n_full - 1, 0)

    accumulate(n_full, 1)
    o = jnp.concatenate([acc_ref[j, :hd] / acc_ref[j, hd:hd + 1] for j in range(pair)], axis=0)
    o_ref[0] = (o * gt_ref[0].astype(F32)).T.astype(o_ref.dtype)


def _fox_attn(qt, k, vt, gt, hd):
    B, heads, nQ = qt.shape[:3]
    TQ = qt.shape[-1]
    nC, CK = k.shape[2], k.shape[3]
    D, Lp = gt.shape[1], gt.shape[2]
    pair = LANE // hd
    return pl.pallas_call(
        functools.partial(_fox_attn_kernel, hd=hd),
        grid=(B, heads // pair, nQ),
        in_specs=[
            pl.BlockSpec((1, pair, 1, LANE, TQ), lambda b, p, q: (b, p, q, 0, 0)),
            pl.BlockSpec((1, pair, nC, CK, LANE), lambda b, p, q: (b, p, 0, 0, 0)),
            pl.BlockSpec((1, pair, nC, vt.shape[3], CK), lambda b, p, q: (b, p, 0, 0, 0)),
            pl.BlockSpec((1, LANE, TQ), lambda b, p, q: (b, p, q)),
        ],
        out_specs=pl.BlockSpec((1, TQ, LANE), lambda b, p, q: (b, q, p)),
        out_shape=jax.ShapeDtypeStruct((B, Lp, D), BF16),
        scratch_shapes=[
            pltpu.VMEM((2, pair, CK, TQ), F32),
            pltpu.VMEM((2, pair, 1, TQ), F32),
            pltpu.VMEM((pair, 1, TQ), F32),
            pltpu.VMEM((pair, vt.shape[3], TQ), F32),
        ],
        compiler_params=_cparams("arbitrary", "arbitrary", "arbitrary"),
        name="fox_attn",
    )(qt, k, vt, gt)


def _ffn_kernel(h_ref, x_ref, wo_ref, g_ref, wg_ref, wu_ref, cw_ref, wd_ref, o_ref, carry_ref):
    @pl.when(pl.program_id(1) == 0)
    def _():
        carry_ref[...] = jnp.zeros_like(carry_ref)

    T = h_ref.shape[1]
    F = wg_ref.shape[1]
    S = FFN_SUB
    n_chunks = F // FFN_CHUNK
    row = lax.broadcasted_iota(jnp.int32, (S, FFN_CHUNK), 0)
    h1s = [h_ref[0, r * S:(r + 1) * S] + _dot(x_ref[0, r * S:(r + 1) * S], wo_ref[...]) for r in range(T // S)]
    xns = [_rms(h1, g_ref[...]).astype(BF16) for h1 in h1s]
    tails = [carry_ref[:, c * FFN_CHUNK:(c + 1) * FFN_CHUNK] for c in range(n_chunks)]

    for r, xn in enumerate(xns):
        def gate_up(c):
            sl = slice(c * FFN_CHUNK, (c + 1) * FFN_CHUNK)
            return _dot(xn, wg_ref[:, sl]), _dot(xn, wu_ref[:, sl])

        acc = h1s[r]
        nxt = gate_up(0)
        for c in range(n_chunks):
            sl = slice(c * FFN_CHUNK, (c + 1) * FFN_CHUNK)
            a, u = nxt
            if c + 1 < n_chunks:
                nxt = gate_up(c + 1)
            prev = tails[c]
            a1 = jnp.where(row == 0, prev[SUBLANE - 1:SUBLANE], pltpu.roll(a, 1, 0))
            a2 = jnp.where(row == 0, prev[SUBLANE - 2:SUBLANE - 1],
                           jnp.where(row == 1, prev[SUBLANE - 1:SUBLANE], pltpu.roll(a, 2, 0)))
            tails[c] = a[S - SUBLANE:S]
            cw = cw_ref[:, sl]
            z = cw[3:4] + cw[0:1] * a2 + cw[1:2] * a1 + cw[2:3] * a
            act = (z * jax.nn.sigmoid(z) * u).astype(BF16)
            acc = acc + _dot(act, wd_ref[sl, :])
        o_ref[0, r * S:(r + 1) * S] = acc

    for c in range(n_chunks):
        carry_ref[:, c * FFN_CHUNK:(c + 1) * FFN_CHUNK] = tails[c]


def _ffn(h, x, w_out, gain, w_gate, w_up, conv_w, conv_b, w_down):
    B, Lp, D = h.shape
    F = w_gate.shape[1]
    cw = jnp.concatenate([conv_w, conv_b[None], jnp.zeros((SUBLANE - conv_w.shape[0] - 1, F), F32)], axis=0)
    return pl.pallas_call(
        _ffn_kernel,
        grid=(B, Lp // FFN_ROWS),
        in_specs=[
            pl.BlockSpec((1, FFN_ROWS, D), lambda b, t: (b, t, 0)),
            pl.BlockSpec((1, FFN_ROWS, D), lambda b, t: (b, t, 0)),
            _resident((D, D)), _resident((1, D)), _resident((D, F)), _resident((D, F)),
            _resident((SUBLANE, F)), _resident((F, D)),
        ],
        out_specs=pl.BlockSpec((1, FFN_ROWS, D), lambda b, t: (b, t, 0)),
        out_shape=jax.ShapeDtypeStruct((B, Lp, D), F32),
        scratch_shapes=[pltpu.VMEM((SUBLANE, F), F32)],
        compiler_params=_cparams("arbitrary", "arbitrary"),
        name="ffn",
    )(h, x, w_out.astype(BF16), gain.reshape(1, D), w_gate.astype(BF16), w_up.astype(BF16), cw,
      w_down.astype(BF16))


def _hgrn_in_kernel(h_ref, g_ref, w_ref, lbs_ref, q_ref, lf_ref, v_ref, sg_ref, *, layer):
    D = h_ref.shape[1]
    xn = _rms(h_ref[...], g_ref[...]).astype(BF16)
    lbs = lbs_ref[...]
    e = jnp.exp(lbs - jnp.max(lbs, axis=0, keepdims=True))
    p = e / jnp.sum(e, axis=0, keepdims=True)
    lb = jnp.sum(p[1:layer + 1], axis=0, keepdims=True) if layer > 0 else jnp.zeros((1, D), F32)

    def proj(i):
        return _dot(xn, w_ref[:, i * D:(i + 1) * D])

    yq = proj(0)
    q_ref[...] = (yq * jax.nn.sigmoid(yq)).astype(q_ref.dtype)
    f = lb + (1.0 - lb) * jax.nn.sigmoid(proj(1))
    lf_ref[...] = jnp.log(f)
    v_ref[...] = proj(2).astype(v_ref.dtype)
    sg_ref[...] = jax.nn.sigmoid(proj(3)).astype(sg_ref.dtype)


def _hgrn_in(h, gain, w, lower_bounds, layer):
    B, Lp, D = h.shape
    rows = B * Lp
    blk = pl.BlockSpec((TOK, D), lambda i: (i, 0))
    outs = pl.pallas_call(
        functools.partial(_hgrn_in_kernel, layer=layer),
        grid=(rows // TOK,),
        in_specs=[blk, _resident((1, D)), _resident(w.shape), _resident(lower_bounds.shape)],
        out_specs=[blk, blk, blk, blk],
        out_shape=[
            jax.ShapeDtypeStruct((rows, D), BF16),
            jax.ShapeDtypeStruct((rows, D), F32),
            jax.ShapeDtypeStruct((rows, D), BF16),
            jax.ShapeDtypeStruct((rows, D), BF16),
        ],
        compiler_params=_cparams("arbitrary"),
        name="hgrn_in",
    )(h.reshape(rows, D), gain.reshape(1, D), w, lower_bounds)
    return [o.reshape(B, Lp, D) for o in outs]


def _hgrn_rec_kernel(q_ref, lf_ref, v_ref, sg_ref, og_ref, o_ref, st_ref, stn_ref, *, dk):
    @pl.when(pl.program_id(1) == 0)
    def _():
        st_ref[...] = jnp.zeros_like(st_ref)

    C = HGRN_CHUNK
    R, D = q_ref.shape[1], q_ref.shape[2]
    sub = HGRN_SUB
    nsub = C // sub
    r_cc = lax.broadcasted_iota(jnp.int32, (C, C), 0)
    c_cc = lax.broadcasted_iota(jnp.int32, (C, C), 1)
    causal = c_cc <= r_cc
    ltri = jnp.where(causal, 1.0, 0.0).astype(BF16)
    row = lax.broadcasted_iota(jnp.int32, (C, dk), 0)

    def run(factored_diag):
        if not factored_diag:
            diag_mask = jnp.logical_and(c_cc >= r_cc - (r_cc & (sub - 1)), causal)
            d_sel = (lax.broadcasted_iota(jnp.int32, (sub * dk, C), 0)
                     - (lax.broadcasted_iota(jnp.int32, (sub * dk, C), 1) & (sub - 1)) * dk)
            sel = jnp.where(jnp.logical_and(d_sel >= 0, d_sel < dk), 1.0, 0.0).astype(BF16)
        n_keys = [(i + 1) * sub if factored_diag else i * sub for i in range(nsub)]

        units = [(hh, ci) for hh in range(D // dk) for ci in range(R // C)]

        def blk(ref, u):
            hh, ci = u
            return ref[0, ci * C:(ci + 1) * C, hh * dk:(hh + 1) * dk]

        b = {}
        for u in units:
            hi, mid, lo = _split3(blk(lf_ref, u))
            b[u] = (_dot(ltri, hi) + _dot(ltri, mid) + _dot(ltri, lo)) * LOG2E

        qe, kk, strips = {}, {}, {}
        for u in units:
            q = blk(q_ref, u).astype(F32)
            kk[u] = 1.0 - jnp.exp(blk(lf_ref, u))
            qe[u] = (q * jnp.exp2(b[u])).astype(BF16)
            parts = []
            for i in range(nsub):
                if n_keys[i] == 0:
                    parts.append(jnp.zeros((sub, C), F32))
                    continue
                bref = b[u][i * sub - 1:i * sub] if i > 0 else jnp.zeros((1, dk), F32)
                qi = q[i * sub:(i + 1) * sub] * jnp.exp2(b[u][i * sub:(i + 1) * sub] - bref)
                ke = kk[u] * jnp.exp2(jnp.where(row < n_keys[i], bref - b[u], NEG))
                parts.append(_dot_nt(qi.astype(BF16), ke.astype(BF16)))
            strips[u] = parts

        intra, kv = {}, {}
        for u in units:
            a = jnp.concatenate(strips[u], axis=0)
            if factored_diag:
                a = jnp.where(causal, a, 0.0)
            else:
                b4 = b[u].reshape(nsub, sub, dk)
                q4 = blk(q_ref, u).astype(F32).reshape(nsub, sub, dk)
                k4 = kk[u].reshape(nsub, sub, dk)
                z = [(q4 * jnp.exp2(jnp.minimum(b4 - b4[:, s:s + 1], 0.0)) * k4[:, s:s + 1])
                     .reshape(C, dk).astype(BF16) for s in range(sub)]
                a = jnp.where(diag_mask, _dot(jnp.concatenate(z, axis=1), sel), a)
            v = blk(v_ref, u)
            intra[u] = _dot(a.astype(BF16), v)
            kv[u] = _dot_tn(v, (kk[u] * jnp.exp2(b[u][C - 1:C] - b[u])).astype(BF16))

        out = {}
        for hh in range(D // dk):
            st = st_ref[hh]
            for ci in range(R // C):
                u = (hh, ci)
                out[u] = _dot_nt(qe[u], st.astype(BF16)) + intra[u]
                st = st * jnp.exp2(b[u][C - 1:C]) + kv[u]
            stn_ref[hh] = st

        for u in units:
            hh, ci = u
            on = _rms(out[u], og_ref[...]) * blk(sg_ref, u).astype(F32)
            o_ref[0, ci * C:(ci + 1) * C, hh * dk:(hh + 1) * dk] = on.astype(o_ref.dtype)

    worst = -jnp.min(jnp.sum(lf_ref[0].reshape(R // sub, sub, D), axis=1))
    run(True)

    @pl.when(worst > HGRN_SAFE_DECAY)
    def _():
        run(False)

    st_ref[...] = stn_ref[...]


def _hgrn_rec(q, lf, v, sg, o_gain):
    B, Lp, D = q.shape
    dk = o_gain.shape[0]
    blk = pl.BlockSpec((1, HGRN_ROWS, D), lambda b, t: (b, t, 0))
    return pl.pallas_call(
        functools.partial(_hgrn_rec_kernel, dk=dk),
        grid=(B, Lp // HGRN_ROWS),
        in_specs=[blk, blk, blk, blk, _resident((1, dk))],
        out_specs=blk,
        out_shape=jax.ShapeDtypeStruct((B, Lp, D), BF16),
        scratch_shapes=[pltpu.VMEM((D // dk, dk, dk), F32)] * 2,
        compiler_params=_cparams("arbitrary", "arbitrary"),
        name="hgrn_rec",
    )(q, lf, v, sg, o_gain.reshape(1, dk))


def kernel(x, meta_tokens, fox_norm, fox_w_in, fox_b_f, fox_q_gain, fox_k_gain, fox_w_out,
           hgrn_norm, hgrn_w_in, hgrn_lower_bounds, hgrn_o_gain, hgrn_w_out,
           ffn_norm, ffn_w_gate, ffn_w_up, ffn_conv_w, ffn_conv_b, ffn_w_down):
    B, S, D = x.shape
    L = S + N_META
    Lp = -(-L // TOK) * TOK
    depth = ffn_norm.shape[0]
    heads = fox_b_f.shape[1]
    hd = D // heads
    assert LANE % hd == 0 and hd + N_EXTRA <= LANE and D % LANE == 0
    assert Lp % FFN_ROWS == 0 and FFN_ROWS % FFN_SUB == 0 and Lp % HGRN_ROWS == 0 and HGRN_ROWS % HGRN_CHUNK == 0 and (Lp // TOK) % KEY_BLOCKS == 0 and (Lp // TOK) % Q_BLOCKS == 0 and KEY_BLOCKS % Q_BLOCKS == 0

    meta = jnp.broadcast_to(meta_tokens[None].astype(x.dtype), (B, N_META, D))
    h = jnp.concatenate([meta, x, jnp.zeros((B, Lp - L, D), x.dtype)], axis=1)

    for layer in range(depth):
        j = layer // 2
        if layer % 2 == 0:
            w = fox_w_in[j]
            w_t = jnp.concatenate([w[:, :3 * D], w[:, 3 * D + heads:]], axis=1).T.astype(BF16)
            wf_t = w[:, 3 * D:3 * D + heads].T.astype(BF16)
            qt, k, vt, gt = _fox_in(h, fox_norm[j], w_t, wf_t, fox_b_f[j],
                                    fox_q_gain[j], fox_k_gain[j], heads)
            k = k.reshape(B, heads, -1, KEY_BLOCKS * TOK, LANE)
            og, w_out = _fox_attn(qt, k, vt, gt, hd), fox_w_out[j]
        else:
            q, lf, v, sg = _hgrn_in(h, hgrn_norm[j], hgrn_w_in[j].astype(BF16),
                                    hgrn_lower_bounds, layer)
            og, w_out = _hgrn_rec(q, lf, v, sg, hgrn_o_gain[j]), hgrn_w_out[j]
        h = _ffn(h, og, w_out, ffn_norm[layer], ffn_w_gate[layer], ffn_w_up[layer],
                 ffn_conv_w[layer], ffn_conv_b[layer], ffn_w_down[layer])
    return h[:, N_META:L]
```

```python
import functools

import jax
import jax.numpy as jnp
from jax import lax
from jax.experimental import pallas as pl
from jax.experimental.pallas import tpu as pltpu

N_META = 16
EPS = 1e-6
NEG = -1e30
LANE = 128
SUBLANE = 8
TOK = 256
KEY_BLOCKS = 3
Q_BLOCKS = 3
FFN_ROWS = 768
FFN_SUB = 256
FFN_CHUNK = 256
HGRN_CHUNK = 64
HGRN_SUB = 16
HGRN_ROWS = 128
HGRN_SAFE_DECAY = 60.0
LOG2E = 1.4426950408889634
N_EXTRA = 16
ATT_SAFE_GAP = 96.0
VMEM_LIMIT = 56 * 1024 * 1024

F32 = jnp.float32
BF16 = jnp.bfloat16


def _cparams(*sem):
    return pltpu.CompilerParams(dimension_semantics=sem, vmem_limit_bytes=VMEM_LIMIT)


def _resident(shape):
    zeros = (0,) * len(shape)
    return pl.BlockSpec(shape, lambda *_: zeros, pipeline_mode=pl.Buffered(1))


def _split3(x):
    hi = x.astype(BF16)
    r = x - hi.astype(F32)
    mid = r.astype(BF16)
    lo = (r - mid.astype(F32)).astype(BF16)
    return hi, mid, lo


def _dot(a, b):
    return jnp.dot(a, b, preferred_element_type=F32)


def _dot_nt(a, b):
    return lax.dot_general(a, b, (((1,), (1,)), ((), ())), preferred_element_type=F32)


def _dot_tn(a, b):
    return lax.dot_general(a, b, (((0,), (0,)), ((), ())), preferred_element_type=F32)


def _rms(x, gain):
    ms = jnp.mean(x * x, axis=-1, keepdims=True)
    return x * lax.rsqrt(ms + EPS) * gain


def _fox_in_kernel(h_ref, g_ref, w_ref, wf_ref, bf_ref, qg_ref, kg_ref,
                   qt_ref, k_ref, vt_ref, gt_ref, mg_ref, carry_ref, kmax_ref, *, heads):
    @pl.when(pl.program_id(1) == 0)
    def _():
        carry_ref[...] = jnp.zeros_like(carry_ref)
        kmax_ref[...] = jnp.zeros_like(kmax_ref)

    T, D = h_ref.shape[1], h_ref.shape[2]
    hd = D // heads
    xnt = _rms(h_ref[0], g_ref[...]).T.astype(BF16)

    def proj(i):
        return _dot(w_ref[i * D:(i + 1) * D, :], xnt)

    fl = _dot(wf_ref[...], xnt) + bf_ref[...]
    lf = jnp.minimum(fl, 0.0) - jnp.log(1.0 + jnp.exp(-jnp.abs(fl)))
    src = lax.broadcasted_iota(jnp.int32, (T, T), 0)
    dst = lax.broadcasted_iota(jnp.int32, (T, T), 1)
    upper = jnp.where(src <= dst, 1.0, 0.0).astype(BF16)
    hi, mid, lo = _split3(lf)
    c = carry_ref[:, 0:1] + (_dot(hi, upper) + _dot(mid, upper) + _dot(lo, upper))
    carry_ref[...] = carry_ref[...] + jnp.sum(lf, axis=1, keepdims=True)
    chi, cmid, clo = (p.astype(F32) for p in _split3(c * LOG2E))

    def head_norm(y, gain_ref, scale):
        y = y.reshape(heads, hd, T)
        ms = jnp.mean(y * y, axis=1, keepdims=True)
        return y * lax.rsqrt(ms + EPS) * (gain_ref[...] * scale)

    qn = head_norm(proj(0), qg_ref, hd ** -0.5 * LOG2E)
    kn = head_norm(proj(1), kg_ref, 1.0)

    knorm = jnp.sqrt(jnp.sum(kn * kn, axis=1))
    kmax = jnp.maximum(kmax_ref[:, 0:1], jnp.max(knorm, axis=1, keepdims=True))
    kmax_ref[...] = jnp.broadcast_to(kmax, kmax_ref.shape)
    bound = jnp.sqrt(jnp.sum(qn * qn, axis=1)) * kmax
    mg_ref[0, 0] = bound
    mg_ref[0, 1] = bound - jnp.sum(qn * kn, axis=1)
    vv = proj(2).reshape(heads, hd, T)
    gt_ref[0] = jax.nn.sigmoid(proj(3)).astype(gt_ref.dtype)

    r = lax.broadcasted_iota(jnp.int32, (N_EXTRA, T), 0)
    zpad = jnp.zeros((LANE - hd - N_EXTRA, T), F32)
    ones_row = jnp.where(r == 0, 1.0, 0.0)
    for hh in range(heads):
        a, b, d = (jnp.broadcast_to(p[hh:hh + 1], (N_EXTRA, T)) for p in (chi, cmid, clo))
        eq = jnp.where(r == 0, a, jnp.where(r == 1, b, jnp.where(r == 2, d, jnp.where(r < 6, 1.0, 0.0))))
        ek = jnp.where(r < 3, 1.0, jnp.where(r == 3, -a, jnp.where(r == 4, -b, jnp.where(r == 5, -d, 0.0))))
        qt_ref[0, hh, 0] = jnp.concatenate([qn[hh], eq, zpad], axis=0).astype(qt_ref.dtype)
        k_ref[0, hh, 0] = jnp.concatenate([kn[hh], ek, zpad], axis=0).T.astype(k_ref.dtype)
        vt_ref[0, hh, 0] = jnp.concatenate([vv[hh], ones_row], axis=0).astype(vt_ref.dtype)


def _fox_in(h, gain, w_t, wf_t, b_f, q_gain, k_gain, heads):
    B, Lp, D = h.shape
    nT = Lp // TOK
    hd = D // heads
    blk_qt = pl.BlockSpec((1, heads, 1, LANE, TOK), lambda b, t: (b, 0, t // Q_BLOCKS, 0, t % Q_BLOCKS))
    blk_vt = pl.BlockSpec((1, heads, 1, hd + N_EXTRA, TOK),
                          lambda b, t: (b, 0, t // KEY_BLOCKS, 0, t % KEY_BLOCKS))
    return pl.pallas_call(
        functools.partial(_fox_in_kernel, heads=heads),
        grid=(B, nT),
        in_specs=[
            pl.BlockSpec((1, TOK, D), lambda b, t: (b, t, 0)),
            _resident((1, D)), _resident(w_t.shape), _resident((heads, D)),
            _resident((heads, 1)), _resident((hd, 1)), _resident((hd, 1)),
        ],
        out_specs=[
            blk_qt,
            pl.BlockSpec((1, heads, 1, TOK, LANE), lambda b, t: (b, 0, t, 0, 0)),
            blk_vt,
            pl.BlockSpec((1, D, TOK), lambda b, t: (b, 0, t)),
            pl.BlockSpec((1, 2, heads, TOK), lambda b, t: (b, 0, 0, t)),
        ],
        out_shape=[
            jax.ShapeDtypeStruct((B, heads, nT // Q_BLOCKS, LANE, Q_BLOCKS * TOK), BF16),
            jax.ShapeDtypeStruct((B, heads, nT, TOK, LANE), BF16),
            jax.ShapeDtypeStruct((B, heads, nT // KEY_BLOCKS, hd + N_EXTRA, KEY_BLOCKS * TOK), BF16),
            jax.ShapeDtypeStruct((B, D, Lp), BF16),
            jax.ShapeDtypeStruct((B, 2, heads, Lp), F32),
        ],
        scratch_shapes=[pltpu.VMEM((heads, LANE), F32)] * 2,
        compiler_params=_cparams("arbitrary", "arbitrary"),
        name="fox_in",
    )(h, gain.reshape(1, D), w_t, wf_t, b_f.reshape(heads, 1),
      q_gain.reshape(hd, 1), k_gain.reshape(hd, 1))


def _fox_attn_kernel(qt_ref, k_ref, vt_ref, gt_ref, mg_ref, o_ref, s_ref, cm_ref, m_ref, acc_ref, *, hd):
    qi = pl.program_id(2)
    T = qt_ref.shape[-1]
    pair = qt_ref.shape[1]
    CK = k_ref.shape[3]
    n_full = (qi * T) // CK

    def logits(c, j, masked):
        s = _dot(k_ref[0, j, c], qt_ref[0, j, 0])
        if masked:
            key = c * CK - qi * T + lax.broadcasted_iota(jnp.int32, (CK, T), 0)
            qry = lax.broadcasted_iota(jnp.int32, (CK, T), 1)
            s = jnp.where(key <= qry, s, NEG)
        return s

    def finalize():
        o = jnp.concatenate([acc_ref[j, :hd] / acc_ref[j, hd:hd + 1] for j in range(pair)], axis=0)
        o_ref[0] = (o * gt_ref[0].astype(F32)).T.astype(o_ref.dtype)

    def bounded():
        acc_ref[...] = jnp.zeros(acc_ref.shape, F32)

        def sweep(chunks):
            units = [(c, j, masked) for c, masked in chunks for j in range(pair)]
            ahead = 2
            s = [logits(*u) for u in units[:ahead]]
            for n, (c, j, _) in enumerate(units):
                if n + ahead < len(units):
                    s.append(logits(*units[n + ahead]))
                p = jnp.exp2(s[n] - mg_ref[0, 0, 0, j:j + 1, :]).astype(BF16)
                acc_ref[j] = acc_ref[j] + _dot(vt_ref[0, j, c], p)

        odd = n_full % 2
        trips = jnp.where(odd == 1, n_full // 2, jnp.maximum(n_full // 2 - 1, 0))

        def body(i, _):
            sweep([(2 * i, False), (2 * i + 1, False)])
            return 0

        lax.fori_loop(0, trips, body, 0)

        @pl.when(n_full == 0)
        def _():
            sweep([(0, True)])

        @pl.when(odd == 1)
        def _():
            sweep([(n_full - 1, False), (n_full, True)])

        @pl.when(jnp.logical_and(odd == 0, n_full > 0))
        def _():
            sweep([(n_full - 2, False), (n_full - 1, False), (n_full, True)])

        finalize()

    def scores(c, slot, masked=False):
        for j in range(pair):
            s = logits(c, j, masked)
            s_ref[slot, j] = s
            cm_ref[slot, j] = jnp.max(s, axis=0, keepdims=True)

    def accumulate(c, slot):
        for j in range(pair):
            m = m_ref[j]
            m_new = jnp.maximum(m, cm_ref[slot, j])
            p = jnp.exp2(s_ref[slot, j] - m_new).astype(BF16)
            acc_ref[j] = acc_ref[j] * jnp.exp2(m - m_new) + _dot(vt_ref[0, j, c], p)
            m_ref[j] = m_new

    def online():
        m_ref[...] = jnp.full(m_ref.shape, NEG, F32)
        acc_ref[...] = jnp.zeros(acc_ref.shape, F32)

        steps = n_full - 1
        odd = steps % 2

        @pl.when(n_full == 0)
        def _():
            scores(0, 1, masked=True)

        @pl.when(jnp.logical_and(n_full > 0, odd == 0))
        def _():
            scores(0, 0)

        @pl.when(jnp.logical_and(n_full > 0, odd == 1))
        def _():
            scores(0, 1)
            scores(1, 0)
            accumulate(0, 1)

        def body(i, _):
            c = odd + 2 * i
            scores(c + 1, 1)
            accumulate(c, 0)
            scores(c + 2, 0)
            accumulate(c + 1, 1)
            return 0

        lax.fori_loop(0, steps // 2, body, 0)

        @pl.when(n_full > 0)
        def _():
            scores(n_full, 1, masked=True)
            accumulate(n_full - 1, 0)

        accumulate(n_full, 1)
        finalize()

    bounded()

    @pl.when(jnp.max(mg_ref[0, 1, 0]) > ATT_SAFE_GAP)
    def _():
        online()


def _fox_attn(qt, k, vt, gt, mg, hd):
    B, heads, nQ = qt.shape[:3]
    TQ = qt.shape[-1]
    nC, CK = k.shape[2], k.shape[3]
    D, Lp = gt.shape[1], gt.shape[2]
    pair = LANE // hd
    return pl.pallas_call(
        functools.partial(_fox_attn_kernel, hd=hd),
        grid=(B, heads // pair, nQ),
        in_specs=[
            pl.BlockSpec((1, pair, 1, LANE, TQ), lambda b, p, q: (b, p, q, 0, 0)),
            pl.BlockSpec((1, pair, nC, CK, LANE), lambda b, p, q: (b, p, 0, 0, 0)),
            pl.BlockSpec((1, pair, nC, vt.shape[3], CK), lambda b, p, q: (b, p, 0, 0, 0)),
            pl.BlockSpec((1, LANE, TQ), lambda b, p, q: (b, p, q)),
            pl.BlockSpec((1, 2, 1, pair, TQ), lambda b, p, q: (b, 0, p, 0, q)),
        ],
        out_specs=pl.BlockSpec((1, TQ, LANE), lambda b, p, q: (b, q, p)),
        out_shape=jax.ShapeDtypeStruct((B, Lp, D), BF16),
        scratch_shapes=[
            pltpu.VMEM((2, pair, CK, TQ), F32),
            pltpu.VMEM((2, pair, 1, TQ), F32),
            pltpu.VMEM((pair, 1, TQ), F32),
            pltpu.VMEM((pair, vt.shape[3], TQ), F32),
        ],
        compiler_params=_cparams("arbitrary", "arbitrary", "arbitrary"),
        name="fox_attn",
    )(qt, k, vt, gt, mg.reshape(B, 2, heads // pair, pair, Lp))


def _ffn_kernel(h_ref, x_ref, wo_ref, g_ref, wg_ref, wu_ref, cw_ref, wd_ref, o_ref, carry_ref):
    @pl.when(pl.program_id(1) == 0)
    def _():
        carry_ref[...] = jnp.zeros_like(carry_ref)

    T = h_ref.shape[1]
    F = wg_ref.shape[1]
    S = FFN_SUB
    n_chunks = F // FFN_CHUNK
    row = lax.broadcasted_iota(jnp.int32, (S, FFN_CHUNK), 0)
    h1s = [h_ref[0, r * S:(r + 1) * S] + _dot(x_ref[0, r * S:(r + 1) * S], wo_ref[...]) for r in range(T // S)]
    xns = [_rms(h1, g_ref[...]).astype(BF16) for h1 in h1s]
    tails = [carry_ref[:, c * FFN_CHUNK:(c + 1) * FFN_CHUNK] for c in range(n_chunks)]

    for r, xn in enumerate(xns):
        def gate_up(c):
            sl = slice(c * FFN_CHUNK, (c + 1) * FFN_CHUNK)
            return _dot(xn, wg_ref[:, sl]), _dot(xn, wu_ref[:, sl])

        acc = h1s[r]
        nxt = gate_up(0)
        for c in range(n_chunks):
            sl = slice(c * FFN_CHUNK, (c + 1) * FFN_CHUNK)
            a, u = nxt
            if c + 1 < n_chunks:
                nxt = gate_up(c + 1)
            prev = tails[c]
            a1 = jnp.where(row == 0, prev[SUBLANE - 1:SUBLANE], pltpu.roll(a, 1, 0))
            a2 = jnp.where(row == 0, prev[SUBLANE - 2:SUBLANE - 1],
                           jnp.where(row == 1, prev[SUBLANE - 1:SUBLANE], pltpu.roll(a, 2, 0)))
            tails[c] = a[S - SUBLANE:S]
            cw = cw_ref[:, sl]
            z = cw[3:4] + cw[0:1] * a2 + cw[1:2] * a1 + cw[2:3] * a
            act = (z * jax.nn.sigmoid(z) * u).astype(BF16)
            acc = acc + _dot(act, wd_ref[sl, :])
        o_ref[0, r * S:(r + 1) * S] = acc

    for c in range(n_chunks):
        carry_ref[:, c * FFN_CHUNK:(c + 1) * FFN_CHUNK] = tails[c]


def _ffn(h, x, w_out, gain, w_gate, w_up, conv_w, conv_b, w_down):
    B, Lp, D = h.shape
    F = w_gate.shape[1]
    cw = jnp.concatenate([conv_w, conv_b[None], jnp.zeros((SUBLANE - conv_w.shape[0] - 1, F), F32)], axis=0)
    return pl.pallas_call(
        _ffn_kernel,
        grid=(B, Lp // FFN_ROWS),
        in_specs=[
            pl.BlockSpec((1, FFN_ROWS, D), lambda b, t: (b, t, 0)),
            pl.BlockSpec((1, FFN_ROWS, D), lambda b, t: (b, t, 0)),
            _resident((D, D)), _resident((1, D)), _resident((D, F)), _resident((D, F)),
            _resident((SUBLANE, F)), _resident((F, D)),
        ],
        out_specs=pl.BlockSpec((1, FFN_ROWS, D), lambda b, t: (b, t, 0)),
        out_shape=jax.ShapeDtypeStruct((B, Lp, D), F32),
        scratch_shapes=[pltpu.VMEM((SUBLANE, F), F32)],
        compiler_params=_cparams("arbitrary", "arbitrary"),
        name="ffn",
    )(h, x, w_out.astype(BF16), gain.reshape(1, D), w_gate.astype(BF16), w_up.astype(BF16), cw,
      w_down.astype(BF16))


def _hgrn_in_kernel(h_ref, g_ref, w_ref, lbs_ref, q_ref, lf_ref, v_ref, sg_ref, *, layer):
    D = h_ref.shape[1]
    xn = _rms(h_ref[...], g_ref[...]).astype(BF16)
    lbs = lbs_ref[...]
    e = jnp.exp(lbs - jnp.max(lbs, axis=0, keepdims=True))
    p = e / jnp.sum(e, axis=0, keepdims=True)
    lb = jnp.sum(p[1:layer + 1], axis=0, keepdims=True) if layer > 0 else jnp.zeros((1, D), F32)

    def proj(i):
        return _dot(xn, w_ref[:, i * D:(i + 1) * D])

    yq = proj(0)
    q_ref[...] = (yq * jax.nn.sigmoid(yq)).astype(q_ref.dtype)
    f = lb + (1.0 - lb) * jax.nn.sigmoid(proj(1))
    lf_ref[...] = jnp.log(f)
    v_ref[...] = proj(2).astype(v_ref.dtype)
    sg_ref[...] = jax.nn.sigmoid(proj(3)).astype(sg_ref.dtype)


def _hgrn_in(h, gain, w, lower_bounds, layer):
    B, Lp, D = h.shape
    rows = B * Lp
    blk = pl.BlockSpec((TOK, D), lambda i: (i, 0))
    outs = pl.pallas_call(
        functools.partial(_hgrn_in_kernel, layer=layer),
        grid=(rows // TOK,),
        in_specs=[blk, _resident((1, D)), _resident(w.shape), _resident(lower_bounds.shape)],
        out_specs=[blk, blk, blk, blk],
        out_shape=[
            jax.ShapeDtypeStruct((rows, D), BF16),
            jax.ShapeDtypeStruct((rows, D), F32),
            jax.ShapeDtypeStruct((rows, D), BF16),
            jax.ShapeDtypeStruct((rows, D), BF16),
        ],
        compiler_params=_cparams("arbitrary"),
        name="hgrn_in",
    )(h.reshape(rows, D), gain.reshape(1, D), w, lower_bounds)
    return [o.reshape(B, Lp, D) for o in outs]


def _hgrn_rec_kernel(q_ref, lf_ref, v_ref, sg_ref, og_ref, o_ref, st_ref, stn_ref, *, dk):
    @pl.when(pl.program_id(1) == 0)
    def _():
        st_ref[...] = jnp.zeros_like(st_ref)

    C = HGRN_CHUNK
    R, D = q_ref.shape[1], q_ref.shape[2]
    sub = HGRN_SUB
    nsub = C // sub
    r_cc = lax.broadcasted_iota(jnp.int32, (C, C), 0)
    c_cc = lax.broadcasted_iota(jnp.int32, (C, C), 1)
    causal = c_cc <= r_cc
    ltri = jnp.where(causal, 1.0, 0.0).astype(BF16)
    row = lax.broadcasted_iota(jnp.int32, (C, dk), 0)

    def run(factored_diag):
        if not factored_diag:
            diag_mask = jnp.logical_and(c_cc >= r_cc - (r_cc & (sub - 1)), causal)
            d_sel = (lax.broadcasted_iota(jnp.int32, (sub * dk, C), 0)
                     - (lax.broadcasted_iota(jnp.int32, (sub * dk, C), 1) & (sub - 1)) * dk)
            sel = jnp.where(jnp.logical_and(d_sel >= 0, d_sel < dk), 1.0, 0.0).astype(BF16)
        n_keys = [(i + 1) * sub if factored_diag else i * sub for i in range(nsub)]

        units = [(hh, ci) for hh in range(D // dk) for ci in range(R // C)]

        def blk(ref, u):
            hh, ci = u
            return ref[0, ci * C:(ci + 1) * C, hh * dk:(hh + 1) * dk]

        b = {}
        for u in units:
            hi, mid, lo = _split3(blk(lf_ref, u))
            b[u] = (_dot(ltri, hi) + _dot(ltri, mid) + _dot(ltri, lo)) * LOG2E

        qe, kk, strips = {}, {}, {}
        for u in units:
            q = blk(q_ref, u).astype(F32)
            kk[u] = 1.0 - jnp.exp(blk(lf_ref, u))
            qe[u] = (q * jnp.exp2(b[u])).astype(BF16)
            parts = []
            for i in range(nsub):
                if n_keys[i] == 0:
                    parts.append(jnp.zeros((sub, C), F32))
                    continue
                bref = b[u][i * sub - 1:i * sub] if i > 0 else jnp.zeros((1, dk), F32)
                qi = q[i * sub:(i + 1) * sub] * jnp.exp2(b[u][i * sub:(i + 1) * sub] - bref)
                ke = kk[u] * jnp.exp2(jnp.where(row < n_keys[i], bref - b[u], NEG))
                parts.append(_dot_nt(qi.astype(BF16), ke.astype(BF16)))
            strips[u] = parts

        intra, kv = {}, {}
        for u in units:
            a = jnp.concatenate(strips[u], axis=0)
            if factored_diag:
                a = jnp.where(causal, a, 0.0)
            else:
                b4 = b[u].reshape(nsub, sub, dk)
                q4 = blk(q_ref, u).astype(F32).reshape(nsub, sub, dk)
                k4 = kk[u].reshape(nsub, sub, dk)
                z = [(q4 * jnp.exp2(jnp.minimum(b4 - b4[:, s:s + 1], 0.0)) * k4[:, s:s + 1])
                     .reshape(C, dk).astype(BF16) for s in range(sub)]
                a = jnp.where(diag_mask, _dot(jnp.concatenate(z, axis=1), sel), a)
            v = blk(v_ref, u)
            intra[u] = _dot(a.astype(BF16), v)
            kv[u] = _dot_tn(v, (kk[u] * jnp.exp2(b[u][C - 1:C] - b[u])).astype(BF16))

        out = {}
        for hh in range(D // dk):
            st = st_ref[hh]
            for ci in range(R // C):
                u = (hh, ci)
                out[u] = _dot_nt(qe[u], st.astype(BF16)) + intra[u]
                st = st * jnp.exp2(b[u][C - 1:C]) + kv[u]
            stn_ref[hh] = st

        for u in units:
            hh, ci = u
            on = _rms(out[u], og_ref[...]) * blk(sg_ref, u).astype(F32)
            o_ref[0, ci * C:(ci + 1) * C, hh * dk:(hh + 1) * dk] = on.astype(o_ref.dtype)

    worst = -jnp.min(jnp.sum(lf_ref[0].reshape(R // sub, sub, D), axis=1))
    run(True)

    @pl.when(worst > HGRN_SAFE_DECAY)
    def _():
        run(False)

    st_ref[...] = stn_ref[...]


def _hgrn_rec(q, lf, v, sg, o_gain):
    B, Lp, D = q.shape
    dk = o_gain.shape[0]
    blk = pl.BlockSpec((1, HGRN_ROWS, D), lambda b, t: (b, t, 0))
    return pl.pallas_call(
        functools.partial(_hgrn_rec_kernel, dk=dk),
        grid=(B, Lp // HGRN_ROWS),
        in_specs=[blk, blk, blk, blk, _resident((1, dk))],
        out_specs=blk,
        out_shape=jax.ShapeDtypeStruct((B, Lp, D), BF16),
        scratch_shapes=[pltpu.VMEM((D // dk, dk, dk), F32)] * 2,
        compiler_params=_cparams("arbitrary", "arbitrary"),
        name="hgrn_rec",
    )(q, lf, v, sg, o_gain.reshape(1, dk))


def kernel(x, meta_tokens, fox_norm, fox_w_in, fox_b_f, fox_q_gain, fox_k_gain, fox_w_out,
           hgrn_norm, hgrn_w_in, hgrn_lower_bounds, hgrn_o_gain, hgrn_w_out,
           ffn_norm, ffn_w_gate, ffn_w_up, ffn_conv_w, ffn_conv_b, ffn_w_down):
    B, S, D = x.shape
    L = S + N_META
    Lp = -(-L // TOK) * TOK
    depth = ffn_norm.shape[0]
    heads = fox_b_f.shape[1]
    hd = D // heads
    assert LANE % hd == 0 and hd + N_EXTRA <= LANE and D % LANE == 0
    assert Lp % FFN_ROWS == 0 and FFN_ROWS % FFN_SUB == 0 and Lp % HGRN_ROWS == 0 and HGRN_ROWS % HGRN_CHUNK == 0 and (Lp // TOK) % KEY_BLOCKS == 0 and (Lp // TOK) % Q_BLOCKS == 0 and KEY_BLOCKS % Q_BLOCKS == 0

    meta = jnp.broadcast_to(meta_tokens[None].astype(x.dtype), (B, N_META, D))
    h = jnp.concatenate([meta, x, jnp.zeros((B, Lp - L, D), x.dtype)], axis=1)

    for layer in range(depth):
        j = layer // 2
        if layer % 2 == 0:
            w = fox_w_in[j]
            w_t = jnp.concatenate([w[:, :3 * D], w[:, 3 * D + heads:]], axis=1).T.astype(BF16)
            wf_t = w[:, 3 * D:3 * D + heads].T.astype(BF16)
            qt, k, vt, gt, mg = _fox_in(h, fox_norm[j], w_t, wf_t, fox_b_f[j],
                                    fox_q_gain[j], fox_k_gain[j], heads)
            k = k.reshape(B, heads, -1, KEY_BLOCKS * TOK, LANE)
            og, w_out = _fox_attn(qt, k, vt, gt, mg, hd), fox_w_out[j]
        else:
            q, lf, v, sg = _hgrn_in(h, hgrn_norm[j], hgrn_w_in[j].astype(BF16),
                                    hgrn_lower_bounds, layer)
            og, w_out = _hgrn_rec(q, lf, v, sg, hgrn_o_gain[j]), hgrn_w_out[j]
        h = _ffn(h, og, w_out, ffn_norm[layer], ffn_w_gate[layer], ffn_w_up[layer],
                 ffn_conv_w[layer], ffn_conv_b[layer], ffn_w_down[layer])
    return h[:, N_META:L]
```

```python
import functools

import jax
import jax.numpy as jnp
from jax import lax
from jax.experimental import pallas as pl
from jax.experimental.pallas import tpu as pltpu

N_META = 16
EPS = 1e-6
NEG = -1e30
LANE = 128
SUBLANE = 8
TOK = 256
KEY_BLOCKS = 3
Q_BLOCKS = 3
FOX_ROWS = 768
FFN_ROWS = 768
FFN_SUB = 256
FFN_CHUNK = 256
HGRN_IN_ROWS = 768
HGRN_CHUNK = 64
HGRN_SUB = 16
HGRN_ROWS = 128
HGRN_SAFE_DECAY = 60.0
LOG2E = 1.4426950408889634
N_EXTRA = 16
ATT_TRIP = 4
ATT_SAFE_GAP = 96.0
VMEM_LIMIT = 56 * 1024 * 1024

F32 = jnp.float32
BF16 = jnp.bfloat16


def _cparams(*sem):
    return pltpu.CompilerParams(dimension_semantics=sem, vmem_limit_bytes=VMEM_LIMIT)


def _resident(shape):
    zeros = (0,) * len(shape)
    return pl.BlockSpec(shape, lambda *_: zeros, pipeline_mode=pl.Buffered(1))


def _split3(x):
    hi = x.astype(BF16)
    r = x - hi.astype(F32)
    mid = r.astype(BF16)
    lo = (r - mid.astype(F32)).astype(BF16)
    return hi, mid, lo


def _dot(a, b):
    return jnp.dot(a, b, preferred_element_type=F32)


def _dot_nt(a, b):
    return lax.dot_general(a, b, (((1,), (1,)), ((), ())), preferred_element_type=F32)


def _dot_tn(a, b):
    return lax.dot_general(a, b, (((0,), (0,)), ((), ())), preferred_element_type=F32)


def _rms(x, gain):
    ms = jnp.mean(x * x, axis=-1, keepdims=True)
    return x * lax.rsqrt(ms + EPS) * gain


def _fox_in_kernel(h_ref, g_ref, w_ref, wf_ref, bf_ref, qg_ref, kg_ref,
                   qt_ref, k_ref, vt_ref, gt_ref, carry_ref, *, heads):
    @pl.when(pl.program_id(1) == 0)
    def _():
        carry_ref[...] = jnp.zeros_like(carry_ref)

    R, D = h_ref.shape[1], h_ref.shape[2]
    T = TOK
    hd = D // heads
    xnts = [_rms(h_ref[0, r * T:(r + 1) * T], g_ref[...]).T.astype(BF16) for r in range(R // T)]

    src = lax.broadcasted_iota(jnp.int32, (T, T), 0)
    dst = lax.broadcasted_iota(jnp.int32, (T, T), 1)
    upper = jnp.where(src <= dst, 1.0, 0.0).astype(BF16)
    row = lax.broadcasted_iota(jnp.int32, (N_EXTRA, T), 0)
    zpad = jnp.zeros((LANE - hd - N_EXTRA, T), F32)
    ones_row = jnp.where(row == 0, 1.0, 0.0)
    carry = carry_ref[:, 0:1]

    def head_norm(y, gain_ref, scale):
        y = y.reshape(heads, hd, T)
        ms = jnp.mean(y * y, axis=1, keepdims=True)
        return y * lax.rsqrt(ms + EPS) * (gain_ref[...] * scale)

    for r, xnt in enumerate(xnts):
        cols = slice(r * T, (r + 1) * T)

        def proj(i):
            return _dot(w_ref[i * D:(i + 1) * D, :], xnt)

        fl = _dot(wf_ref[...], xnt) + bf_ref[...]
        lf = jnp.minimum(fl, 0.0) - jnp.log(1.0 + jnp.exp(-jnp.abs(fl)))
        hi, mid, lo = _split3(lf)
        c = carry + (_dot(hi, upper) + _dot(mid, upper) + _dot(lo, upper))
        carry = carry + jnp.sum(lf, axis=1, keepdims=True)
        chi, cmid, clo = (p.astype(F32) for p in _split3(c * LOG2E))

        qn = head_norm(proj(0), qg_ref, hd ** -0.5 * LOG2E)
        kn = head_norm(proj(1), kg_ref, 1.0)
        vv = proj(2).reshape(heads, hd, T)
        gt_ref[0, :, cols] = jax.nn.sigmoid(proj(3)).astype(gt_ref.dtype)

        for hh in range(heads):
            a, b, d = (jnp.broadcast_to(p[hh:hh + 1], (N_EXTRA, T)) for p in (chi, cmid, clo))
            eq = jnp.where(row == 0, a, jnp.where(row == 1, b, jnp.where(row == 2, d, jnp.where(row < 6, 1.0, 0.0))))
            ek = jnp.where(row < 3, 1.0, jnp.where(row == 3, -a, jnp.where(row == 4, -b, jnp.where(row == 5, -d, 0.0))))
            qt_ref[0, hh, 0, :, cols] = jnp.concatenate([qn[hh], eq, zpad], axis=0).astype(qt_ref.dtype)
            k_ref[0, hh, r] = jnp.concatenate([kn[hh], ek, zpad], axis=0).T.astype(k_ref.dtype)
            vt_ref[0, hh, 0, :, cols] = jnp.concatenate([vv[hh], ones_row], axis=0).astype(vt_ref.dtype)

    carry_ref[...] = jnp.broadcast_to(carry, carry_ref.shape)


def _fox_in(h, gain, w_t, wf_t, b_f, q_gain, k_gain, heads):
    B, Lp, D = h.shape
    hd = D // heads
    R = FOX_ROWS
    nR, sub = Lp // R, R // TOK
    assert R == Q_BLOCKS * TOK == KEY_BLOCKS * TOK
    return pl.pallas_call(
        functools.partial(_fox_in_kernel, heads=heads),
        grid=(B, nR),
        in_specs=[
            pl.BlockSpec((1, R, D), lambda b, t: (b, t, 0)),
            _resident((1, D)), _resident(w_t.shape), _resident((heads, D)),
            _resident((heads, 1)), _resident((hd, 1)), _resident((hd, 1)),
        ],
        out_specs=[
            pl.BlockSpec((1, heads, 1, LANE, R), lambda b, t: (b, 0, t, 0, 0)),
            pl.BlockSpec((1, heads, sub, TOK, LANE), lambda b, t: (b, 0, t, 0, 0)),
            pl.BlockSpec((1, heads, 1, hd + N_EXTRA, R), lambda b, t: (b, 0, t, 0, 0)),
            pl.BlockSpec((1, D, R), lambda b, t: (b, 0, t)),
        ],
        out_shape=[
            jax.ShapeDtypeStruct((B, heads, nR, LANE, R), BF16),
            jax.ShapeDtypeStruct((B, heads, nR * sub, TOK, LANE), BF16),
            jax.ShapeDtypeStruct((B, heads, nR, hd + N_EXTRA, R), BF16),
            jax.ShapeDtypeStruct((B, D, Lp), BF16),
        ],
        scratch_shapes=[pltpu.VMEM((heads, LANE), F32)],
        compiler_params=_cparams("arbitrary", "arbitrary"),
        name="fox_in",
    )(h, gain.reshape(1, D), w_t, wf_t, b_f.reshape(heads, 1),
      q_gain.reshape(hd, 1), k_gain.reshape(hd, 1))


def _fox_attn_kernel(bound_ref, qt_ref, k_ref, vt_ref, gt_ref, o_ref, s_ref, cm_ref, m_ref, acc_ref, *, hd):
    qi = pl.program_id(2)
    T = qt_ref.shape[-1]
    pair = qt_ref.shape[1]
    CK = k_ref.shape[3]
    n_full = (qi * T) // CK
    bound = bound_ref[0]

    def logits(c, j, masked):
        s = _dot(k_ref[0, j, c], qt_ref[0, j, 0])
        if masked:
            key = c * CK - qi * T + lax.broadcasted_iota(jnp.int32, (CK, T), 0)
            qry = lax.broadcasted_iota(jnp.int32, (CK, T), 1)
            s = jnp.where(key <= qry, s, NEG)
        return s

    def finalize():
        o = jnp.concatenate([acc_ref[j, :hd] / acc_ref[j, hd:hd + 1] for j in range(pair)], axis=0)
        o_ref[0] = (o * gt_ref[0].astype(F32)).T.astype(o_ref.dtype)

    def bounded():
        acc_ref[...] = jnp.zeros(acc_ref.shape, F32)

        def sweep(chunks):
            units = [(c, j, masked) for c, masked in chunks for j in range(pair)]
            ahead = 2
            s = [logits(*u) for u in units[:ahead]]
            for n, (c, j, _) in enumerate(units):
                if n + ahead < len(units):
                    s.append(logits(*units[n + ahead]))
                p = jnp.exp2(s[n] - bound).astype(BF16)
                acc_ref[j] = acc_ref[j] + _dot(vt_ref[0, j, c], p)

        rem = n_full % ATT_TRIP

        def body(i, _):
            sweep([(ATT_TRIP * i + d, False) for d in range(ATT_TRIP)])
            return 0

        lax.fori_loop(0, n_full // ATT_TRIP, body, 0)

        for r in range(ATT_TRIP):
            @pl.when(rem == r)
            def _(r=r):
                sweep([(n_full - r + d, False) for d in range(r)] + [(n_full, True)])

        finalize()

    def scores(c, slot, masked=False):
        for j in range(pair):
            s = logits(c, j, masked)
            s_ref[slot, j] = s
            cm_ref[slot, j] = jnp.max(s, axis=0, keepdims=True)

    def accumulate(c, slot):
        for j in range(pair):
            m = m_ref[j]
            m_new = jnp.maximum(m, cm_ref[slot, j])
            p = jnp.exp2(s_ref[slot, j] - m_new).astype(BF16)
            acc_ref[j] = acc_ref[j] * jnp.exp2(m - m_new) + _dot(vt_ref[0, j, c], p)
            m_ref[j] = m_new

    def online():
        m_ref[...] = jnp.full(m_ref.shape, NEG, F32)
        acc_ref[...] = jnp.zeros(acc_ref.shape, F32)

        steps = n_full - 1
        odd = steps % 2

        @pl.when(n_full == 0)
        def _():
            scores(0, 1, masked=True)

        @pl.when(jnp.logical_and(n_full > 0, odd == 0))
        def _():
            scores(0, 0)

        @pl.when(jnp.logical_and(n_full > 0, odd == 1))
        def _():
            scores(0, 1)
            scores(1, 0)
            accumulate(0, 1)

        def body(i, _):
            c = odd + 2 * i
            scores(c + 1, 1)
            accumulate(c, 0)
            scores(c + 2, 0)
            accumulate(c + 1, 1)
            return 0

        lax.fori_loop(0, steps // 2, body, 0)

        @pl.when(n_full > 0)
        def _():
            scores(n_full, 1, masked=True)
            accumulate(n_full - 1, 0)

        accumulate(n_full, 1)
        finalize()

    bounded()

    @pl.when(2.0 * bound > ATT_SAFE_GAP)
    def _():
        online()


def _fox_attn(qt, k, vt, gt, q_gain, k_gain, hd):
    B, heads, nQ = qt.shape[:3]
    TQ = qt.shape[-1]
    nC, CK = k.shape[2], k.shape[3]
    D, Lp = gt.shape[1], gt.shape[2]
    pair = LANE // hd
    bound = jnp.max(jnp.abs(q_gain)) * jnp.max(jnp.abs(k_gain)) * (hd ** 0.5 * LOG2E)
    return pl.pallas_call(
        functools.partial(_fox_attn_kernel, hd=hd),
        grid=(B, heads // pair, nQ),
        in_specs=[
            pl.BlockSpec(memory_space=pltpu.SMEM),
            pl.BlockSpec((1, pair, 1, LANE, TQ), lambda b, p, q: (b, p, q, 0, 0)),
            pl.BlockSpec((1, pair, nC, CK, LANE), lambda b, p, q: (b, p, 0, 0, 0)),
            pl.BlockSpec((1, pair, nC, vt.shape[3], CK), lambda b, p, q: (b, p, 0, 0, 0)),
            pl.BlockSpec((1, LANE, TQ), lambda b, p, q: (b, p, q)),
        ],
        out_specs=pl.BlockSpec((1, TQ, LANE), lambda b, p, q: (b, q, p)),
        out_shape=jax.ShapeDtypeStruct((B, Lp, D), BF16),
        scratch_shapes=[
            pltpu.VMEM((2, pair, CK, TQ), F32),
            pltpu.VMEM((2, pair, 1, TQ), F32),
            pltpu.VMEM((pair, 1, TQ), F32),
            pltpu.VMEM((pair, vt.shape[3], TQ), F32),
        ],
        compiler_params=_cparams("arbitrary", "arbitrary", "arbitrary"),
        name="fox_attn",
    )(bound.reshape(1).astype(F32), qt, k, vt, gt)


def _ffn_kernel(h_ref, x_ref, wo_ref, g_ref, wg_ref, wu_ref, cw_ref, wd_ref, o_ref, carry_ref):
    @pl.when(pl.program_id(1) == 0)
    def _():
        carry_ref[...] = jnp.zeros_like(carry_ref)

    T = h_ref.shape[1]
    F = wg_ref.shape[1]
    S = FFN_SUB
    n_chunks = F // FFN_CHUNK
    row = lax.broadcasted_iota(jnp.int32, (S, FFN_CHUNK), 0)
    h1s = [h_ref[0, r * S:(r + 1) * S] + _dot(x_ref[0, r * S:(r + 1) * S], wo_ref[...]) for r in range(T // S)]
    xns = [_rms(h1, g_ref[...]).astype(BF16) for h1 in h1s]
    tails = [carry_ref[:, c * FFN_CHUNK:(c + 1) * FFN_CHUNK] for c in range(n_chunks)]

    for r, xn in enumerate(xns):
        def gate_up(c):
            sl = slice(c * FFN_CHUNK, (c + 1) * FFN_CHUNK)
            return _dot(xn, wg_ref[:, sl]), _dot(xn, wu_ref[:, sl])

        acc = h1s[r]
        nxt = gate_up(0)
        for c in range(n_chunks):
            sl = slice(c * FFN_CHUNK, (c + 1) * FFN_CHUNK)
            a, u = nxt
            if c + 1 < n_chunks:
                nxt = gate_up(c + 1)
            prev = tails[c]
            a1 = jnp.where(row == 0, prev[SUBLANE - 1:SUBLANE], pltpu.roll(a, 1, 0))
            a2 = jnp.where(row == 0, prev[SUBLANE - 2:SUBLANE - 1],
                           jnp.where(row == 1, prev[SUBLANE - 1:SUBLANE], pltpu.roll(a, 2, 0)))
            tails[c] = a[S - SUBLANE:S]
            cw = cw_ref[:, sl]
            z = cw[3:4] + cw[0:1] * a2 + cw[1:2] * a1 + cw[2:3] * a
            act = (z * jax.nn.sigmoid(z) * u).astype(BF16)
            acc = acc + _dot(act, wd_ref[sl, :])
        o_ref[0, r * S:(r + 1) * S] = acc

    for c in range(n_chunks):
        carry_ref[:, c * FFN_CHUNK:(c + 1) * FFN_CHUNK] = tails[c]


def _ffn(h, x, w_out, gain, w_gate, w_up, conv_w, conv_b, w_down):
    B, Lp, D = h.shape
    F = w_gate.shape[1]
    cw = jnp.concatenate([conv_w, conv_b[None], jnp.zeros((SUBLANE - conv_w.shape[0] - 1, F), F32)], axis=0)
    return pl.pallas_call(
        _ffn_kernel,
        grid=(B, Lp // FFN_ROWS),
        in_specs=[
            pl.BlockSpec((1, FFN_ROWS, D), lambda b, t: (b, t, 0)),
            pl.BlockSpec((1, FFN_ROWS, D), lambda b, t: (b, t, 0)),
            _resident((D, D)), _resident((1, D)), _resident((D, F)), _resident((D, F)),
            _resident((SUBLANE, F)), _resident((F, D)),
        ],
        out_specs=pl.BlockSpec((1, FFN_ROWS, D), lambda b, t: (b, t, 0)),
        out_shape=jax.ShapeDtypeStruct((B, Lp, D), F32),
        scratch_shapes=[pltpu.VMEM((SUBLANE, F), F32)],
        compiler_params=_cparams("arbitrary", "arbitrary"),
        name="ffn",
    )(h, x, w_out.astype(BF16), gain.reshape(1, D), w_gate.astype(BF16), w_up.astype(BF16), cw,
      w_down.astype(BF16))


def _hgrn_in_kernel(h_ref, g_ref, w_ref, lbs_ref, q_ref, lf_ref, v_ref, sg_ref, *, layer):
    R, D = h_ref.shape
    T = TOK
    lbs = lbs_ref[...]
    e = jnp.exp(lbs - jnp.max(lbs, axis=0, keepdims=True))
    p = e / jnp.sum(e, axis=0, keepdims=True)
    lb = jnp.sum(p[1:layer + 1], axis=0, keepdims=True) if layer > 0 else jnp.zeros((1, D), F32)

    xns = [_rms(h_ref[r * T:(r + 1) * T], g_ref[...]).astype(BF16) for r in range(R // T)]
    for r, xn in enumerate(xns):
        rows = slice(r * T, (r + 1) * T)

        def proj(i):
            return _dot(xn, w_ref[:, i * D:(i + 1) * D])

        yq = proj(0)
        q_ref[rows] = (yq * jax.nn.sigmoid(yq)).astype(q_ref.dtype)
        f = lb + (1.0 - lb) * jax.nn.sigmoid(proj(1))
        lf_ref[rows] = jnp.log(f)
        v_ref[rows] = proj(2).astype(v_ref.dtype)
        sg_ref[rows] = jax.nn.sigmoid(proj(3)).astype(sg_ref.dtype)


def _hgrn_in(h, gain, w, lower_bounds, layer):
    B, Lp, D = h.shape
    rows = B * Lp
    blk = pl.BlockSpec((HGRN_IN_ROWS, D), lambda i: (i, 0))
    outs = pl.pallas_call(
        functools.partial(_hgrn_in_kernel, layer=layer),
        grid=(rows // HGRN_IN_ROWS,),
        in_specs=[blk, _resident((1, D)), _resident(w.shape), _resident(lower_bounds.shape)],
        out_specs=[blk, blk, blk, blk],
        out_shape=[
            jax.ShapeDtypeStruct((rows, D), BF16),
            jax.ShapeDtypeStruct((rows, D), F32),
            jax.ShapeDtypeStruct((rows, D), BF16),
            jax.ShapeDtypeStruct((rows, D), BF16),
        ],
        compiler_params=_cparams("arbitrary"),
        name="hgrn_in",
    )(h.reshape(rows, D), gain.reshape(1, D), w, lower_bounds)
    return [o.reshape(B, Lp, D) for o in outs]


def _hgrn_rec_kernel(q_ref, lf_ref, v_ref, sg_ref, og_ref, o_ref, st_ref, stn_ref, *, dk):
    @pl.when(pl.program_id(1) == 0)
    def _():
        st_ref[...] = jnp.zeros_like(st_ref)

    C = HGRN_CHUNK
    R, D = q_ref.shape[1], q_ref.shape[2]
    sub = HGRN_SUB
    nsub = C // sub
    r_cc = lax.broadcasted_iota(jnp.int32, (C, C), 0)
    c_cc = lax.broadcasted_iota(jnp.int32, (C, C), 1)
    causal = c_cc <= r_cc
    ltri = jnp.where(causal, 1.0, 0.0).astype(BF16)
    row = lax.broadcasted_iota(jnp.int32, (C, dk), 0)

    def run(factored_diag):
        if not factored_diag:
            diag_mask = jnp.logical_and(c_cc >= r_cc - (r_cc & (sub - 1)), causal)
            d_sel = (lax.broadcasted_iota(jnp.int32, (sub * dk, C), 0)
                     - (lax.broadcasted_iota(jnp.int32, (sub * dk, C), 1) & (sub - 1)) * dk)
            sel = jnp.where(jnp.logical_and(d_sel >= 0, d_sel < dk), 1.0, 0.0).astype(BF16)
        n_keys = [(i + 1) * sub if factored_diag else i * sub for i in range(nsub)]

        units = [(hh, ci) for hh in range(D // dk) for ci in range(R // C)]

        def blk(ref, u):
            hh, ci = u
            return ref[0, ci * C:(ci + 1) * C, hh * dk:(hh + 1) * dk]

        b = {}
        for u in units:
            hi, mid, lo = _split3(blk(lf_ref, u))
            b[u] = (_dot(ltri, hi) + _dot(ltri, mid) + _dot(ltri, lo)) * LOG2E

        qe, kk, strips = {}, {}, {}
        for u in units:
            q = blk(q_ref, u).astype(F32)
            kk[u] = 1.0 - jnp.exp(blk(lf_ref, u))
            qe[u] = (q * jnp.exp2(b[u])).astype(BF16)
            parts = []
            for i in range(nsub):
                if n_keys[i] == 0:
                    parts.append(jnp.zeros((sub, C), F32))
                    continue
                bref = b[u][i * sub - 1:i * sub] if i > 0 else jnp.zeros((1, dk), F32)
                qi = q[i * sub:(i + 1) * sub] * jnp.exp2(b[u][i * sub:(i + 1) * sub] - bref)
                ke = kk[u] * jnp.exp2(jnp.where(row < n_keys[i], bref - b[u], NEG))
                parts.append(_dot_nt(qi.astype(BF16), ke.astype(BF16)))
            strips[u] = parts

        intra, kv = {}, {}
        for u in units:
            a = jnp.concatenate(strips[u], axis=0)
            if factored_diag:
                a = jnp.where(causal, a, 0.0)
            else:
                b4 = b[u].reshape(nsub, sub, dk)
                q4 = blk(q_ref, u).astype(F32).reshape(nsub, sub, dk)
                k4 = kk[u].reshape(nsub, sub, dk)
                z = [(q4 * jnp.exp2(jnp.minimum(b4 - b4[:, s:s + 1], 0.0)) * k4[:, s:s + 1])
                     .reshape(C, dk).astype(BF16) for s in range(sub)]
                a = jnp.where(diag_mask, _dot(jnp.concatenate(z, axis=1), sel), a)
            v = blk(v_ref, u)
            intra[u] = _dot(a.astype(BF16), v)
            kv[u] = _dot_tn(v, (kk[u] * jnp.exp2(b[u][C - 1:C] - b[u])).astype(BF16))

        out = {}
        for hh in range(D // dk):
            st = st_ref[hh]
            for ci in range(R // C):
                u = (hh, ci)
                out[u] = _dot_nt(qe[u], st.astype(BF16)) + intra[u]
                st = st * jnp.exp2(b[u][C - 1:C]) + kv[u]
            stn_ref[hh] = st

        for u in units:
            hh, ci = u
            on = _rms(out[u], og_ref[...]) * blk(sg_ref, u).astype(F32)
            o_ref[0, ci * C:(ci + 1) * C, hh * dk:(hh + 1) * dk] = on.astype(o_ref.dtype)

    worst = -jnp.min(jnp.sum(lf_ref[0].reshape(R // sub, sub, D), axis=1))
    run(True)

    @pl.when(worst > HGRN_SAFE_DECAY)
    def _():
        run(False)

    st_ref[...] = stn_ref[...]


def _hgrn_rec(q, lf, v, sg, o_gain):
    B, Lp, D = q.shape
    dk = o_gain.shape[0]
    blk = pl.BlockSpec((1, HGRN_ROWS, D), lambda b, t: (b, t, 0))
    return pl.pallas_call(
        functools.partial(_hgrn_rec_kernel, dk=dk),
        grid=(B, Lp // HGRN_ROWS),
        in_specs=[blk, blk, blk, blk, _resident((1, dk))],
        out_specs=blk,
        out_shape=jax.ShapeDtypeStruct((B, Lp, D), BF16),
        scratch_shapes=[pltpu.VMEM((D // dk, dk, dk), F32)] * 2,
        compiler_params=_cparams("arbitrary", "arbitrary"),
        name="hgrn_rec",
    )(q, lf, v, sg, o_gain.reshape(1, dk))


def kernel(x, meta_tokens, fox_norm, fox_w_in, fox_b_f, fox_q_gain, fox_k_gain, fox_w_out,
           hgrn_norm, hgrn_w_in, hgrn_lower_bounds, hgrn_o_gain, hgrn_w_out,
           ffn_norm, ffn_w_gate, ffn_w_up, ffn_conv_w, ffn_conv_b, ffn_w_down):
    B, S, D = x.shape
    L = S + N_META
    Lp = -(-L // TOK) * TOK
    depth = ffn_norm.shape[0]
    heads = fox_b_f.shape[1]
    hd = D // heads
    assert LANE % hd == 0 and hd + N_EXTRA <= LANE and D % LANE == 0
    assert Lp % FOX_ROWS == 0 and (B * Lp) % HGRN_IN_ROWS == 0 and HGRN_IN_ROWS % TOK == 0
    assert Lp % FFN_ROWS == 0 and FFN_ROWS % FFN_SUB == 0 and Lp % HGRN_ROWS == 0 and HGRN_ROWS % HGRN_CHUNK == 0 and (Lp // TOK) % KEY_BLOCKS == 0 and (Lp // TOK) % Q_BLOCKS == 0 and KEY_BLOCKS % Q_BLOCKS == 0

    meta = jnp.broadcast_to(meta_tokens[None].astype(x.dtype), (B, N_META, D))
    h = jnp.concatenate([meta, x, jnp.zeros((B, Lp - L, D), x.dtype)], axis=1)

    for layer in range(depth):
        j = layer // 2
        if layer % 2 == 0:
            w = fox_w_in[j]
            w_t = jnp.concatenate([w[:, :3 * D], w[:, 3 * D + heads:]], axis=1).T.astype(BF16)
            wf_t = w[:, 3 * D:3 * D + heads].T.astype(BF16)
            qt, k, vt, gt = _fox_in(h, fox_norm[j], w_t, wf_t, fox_b_f[j],
                                    fox_q_gain[j], fox_k_gain[j], heads)
            k = k.reshape(B, heads, -1, KEY_BLOCKS * TOK, LANE)
            og, w_out = _fox_attn(qt, k, vt, gt, fox_q_gain[j], fox_k_gain[j], hd), fox_w_out[j]
        else:
            q, lf, v, sg = _hgrn_in(h, hgrn_norm[j], hgrn_w_in[j].astype(BF16),
                                    hgrn_lower_bounds, layer)
            og, w_out = _hgrn_rec(q, lf, v, sg, hgrn_o_gain[j]), hgrn_w_out[j]
        h = _ffn(h, og, w_out, ffn_norm[layer], ffn_w_gate[layer], ffn_w_up[layer],
                 ffn_conv_w[layer], ffn_conv_b[layer], ffn_w_down[layer])
    return h[:, N_META:L]
```

```python
import functools

import jax
import jax.numpy as jnp
from jax import lax
from jax.experimental import pallas as pl
from jax.experimental.pallas import tpu as pltpu

N_META = 16
EPS = 1e-6
NEG = -1e30
LANE = 128
SUBLANE = 8
TOK = 256
KEY_BLOCKS = 3
Q_BLOCKS = 3
FOX_ROWS = 768
FFN_ROWS = 768
FFN_SUB = 256
FFN_CHUNK = 256
HGRN_IN_ROWS = 768
HGRN_CHUNK = 64
HGRN_SUB = 16
HGRN_ROWS = 256
HGRN_RERUN_ROWS = 128
HGRN_SAFE_DECAY = 60.0
LOG2E = 1.4426950408889634
N_EXTRA = 16
ATT_TRIP = 4
ATT_SAFE_GAP = 96.0
VMEM_LIMIT = 56 * 1024 * 1024

F32 = jnp.float32
BF16 = jnp.bfloat16


def _cparams(*sem):
    return pltpu.CompilerParams(dimension_semantics=sem, vmem_limit_bytes=VMEM_LIMIT)


def _resident(shape):
    zeros = (0,) * len(shape)
    return pl.BlockSpec(shape, lambda *_: zeros, pipeline_mode=pl.Buffered(1))


def _split3(x):
    hi = x.astype(BF16)
    r = x - hi.astype(F32)
    mid = r.astype(BF16)
    lo = (r - mid.astype(F32)).astype(BF16)
    return hi, mid, lo


def _dot(a, b):
    return jnp.dot(a, b, preferred_element_type=F32)


def _dot_nt(a, b):
    return lax.dot_general(a, b, (((1,), (1,)), ((), ())), preferred_element_type=F32)


def _dot_tn(a, b):
    return lax.dot_general(a, b, (((0,), (0,)), ((), ())), preferred_element_type=F32)


def _rms(x, gain):
    ms = jnp.mean(x * x, axis=-1, keepdims=True)
    return x * lax.rsqrt(ms + EPS) * gain


def _fox_in_kernel(h_ref, g_ref, w_ref, wf_ref, bf_ref, qg_ref, kg_ref,
                   qt_ref, k_ref, vt_ref, gt_ref, carry_ref, *, heads):
    @pl.when(pl.program_id(1) == 0)
    def _():
        carry_ref[...] = jnp.zeros_like(carry_ref)

    R, D = h_ref.shape[1], h_ref.shape[2]
    T = TOK
    hd = D // heads
    xnts = [_rms(h_ref[0, r * T:(r + 1) * T], g_ref[...]).T.astype(BF16) for r in range(R // T)]

    src = lax.broadcasted_iota(jnp.int32, (T, T), 0)
    dst = lax.broadcasted_iota(jnp.int32, (T, T), 1)
    upper = jnp.where(src <= dst, 1.0, 0.0).astype(BF16)
    row = lax.broadcasted_iota(jnp.int32, (N_EXTRA, T), 0)
    zpad = jnp.zeros((LANE - hd - N_EXTRA, T), F32)
    ones_row = jnp.where(row == 0, 1.0, 0.0)
    carry = carry_ref[:, 0:1]

    def head_norm(y, gain_ref, scale):
        y = y.reshape(heads, hd, T)
        ms = jnp.mean(y * y, axis=1, keepdims=True)
        return y * lax.rsqrt(ms + EPS) * (gain_ref[...] * scale)

    for r, xnt in enumerate(xnts):
        cols = slice(r * T, (r + 1) * T)

        def proj(i):
            return _dot(w_ref[i * D:(i + 1) * D, :], xnt)

        fl = _dot(wf_ref[...], xnt) + bf_ref[...]
        lf = jnp.minimum(fl, 0.0) - jnp.log(1.0 + jnp.exp(-jnp.abs(fl)))
        hi, mid, lo = _split3(lf)
        c = carry + (_dot(hi, upper) + _dot(mid, upper) + _dot(lo, upper))
        carry = carry + jnp.sum(lf, axis=1, keepdims=True)
        chi, cmid, clo = (p.astype(F32) for p in _split3(c * LOG2E))

        qn = head_norm(proj(0), qg_ref, hd ** -0.5 * LOG2E)
        kn = head_norm(proj(1), kg_ref, 1.0)
        vv = proj(2).reshape(heads, hd, T)
        gt_ref[0, :, cols] = jax.nn.sigmoid(proj(3)).astype(gt_ref.dtype)

        for hh in range(heads):
            a, b, d = (jnp.broadcast_to(p[hh:hh + 1], (N_EXTRA, T)) for p in (chi, cmid, clo))
            eq = jnp.where(row == 0, a, jnp.where(row == 1, b, jnp.where(row == 2, d, jnp.where(row < 6, 1.0, 0.0))))
            ek = jnp.where(row < 3, 1.0, jnp.where(row == 3, -a, jnp.where(row == 4, -b, jnp.where(row == 5, -d, 0.0))))
            qt_ref[0, hh, 0, :, cols] = jnp.concatenate([qn[hh], eq, zpad], axis=0).astype(qt_ref.dtype)
            k_ref[0, hh, r] = jnp.concatenate([kn[hh], ek, zpad], axis=0).T.astype(k_ref.dtype)
            vt_ref[0, hh, 0, :, cols] = jnp.concatenate([vv[hh], ones_row], axis=0).astype(vt_ref.dtype)

    carry_ref[...] = jnp.broadcast_to(carry, carry_ref.shape)


def _fox_in(h, gain, w_t, wf_t, b_f, q_gain, k_gain, heads):
    B, Lp, D = h.shape
    hd = D // heads
    R = FOX_ROWS
    nR, sub = Lp // R, R // TOK
    assert R == Q_BLOCKS * TOK == KEY_BLOCKS * TOK
    return pl.pallas_call(
        functools.partial(_fox_in_kernel, heads=heads),
        grid=(B, nR),
        in_specs=[
            pl.BlockSpec((1, R, D), lambda b, t: (b, t, 0)),
            _resident((1, D)), _resident(w_t.shape), _resident((heads, D)),
            _resident((heads, 1)), _resident((hd, 1)), _resident((hd, 1)),
        ],
        out_specs=[
            pl.BlockSpec((1, heads, 1, LANE, R), lambda b, t: (b, 0, t, 0, 0)),
            pl.BlockSpec((1, heads, sub, TOK, LANE), lambda b, t: (b, 0, t, 0, 0)),
            pl.BlockSpec((1, heads, 1, hd + N_EXTRA, R), lambda b, t: (b, 0, t, 0, 0)),
            pl.BlockSpec((1, D, R), lambda b, t: (b, 0, t)),
        ],
        out_shape=[
            jax.ShapeDtypeStruct((B, heads, nR, LANE, R), BF16),
            jax.ShapeDtypeStruct((B, heads, nR * sub, TOK, LANE), BF16),
            jax.ShapeDtypeStruct((B, heads, nR, hd + N_EXTRA, R), BF16),
            jax.ShapeDtypeStruct((B, D, Lp), BF16),
        ],
        scratch_shapes=[pltpu.VMEM((heads, LANE), F32)],
        compiler_params=_cparams("arbitrary", "arbitrary"),
        name="fox_in",
    )(h, gain.reshape(1, D), w_t, wf_t, b_f.reshape(heads, 1),
      q_gain.reshape(hd, 1), k_gain.reshape(hd, 1))


def _fox_attn_kernel(bound_ref, qt_ref, k_ref, vt_ref, gt_ref, o_ref, s_ref, cm_ref, m_ref, acc_ref, *, hd):
    qi = pl.program_id(2)
    T = qt_ref.shape[-1]
    pair = qt_ref.shape[1]
    CK = k_ref.shape[3]
    n_full = (qi * T) // CK
    assert CK == T
    bound = bound_ref[0]

    def logits(c, j, masked):
        s = _dot(k_ref[0, j, c], qt_ref[0, j, 0])
        if masked:
            key = c * CK - qi * T + lax.broadcasted_iota(jnp.int32, (CK, T), 0)
            qry = lax.broadcasted_iota(jnp.int32, (CK, T), 1)
            s = jnp.where(key <= qry, s, NEG)
        return s

    def finalize():
        o = jnp.concatenate([acc_ref[j, :hd] / acc_ref[j, hd:hd + 1] for j in range(pair)], axis=0)
        o_ref[0] = (o * gt_ref[0].astype(F32)).T.astype(o_ref.dtype)

    def bounded():
        acc_ref[...] = jnp.zeros(acc_ref.shape, F32)

        def diag_logits(c, j):
            parts = []
            for lo in range(0, CK, TOK):
                s = _dot(k_ref[0, j, c, lo:lo + TOK, :], qt_ref[0, j, 0, :, lo:])
                key = lax.broadcasted_iota(jnp.int32, s.shape, 0)
                qry = lax.broadcasted_iota(jnp.int32, s.shape, 1)
                parts.append(jnp.where(key <= qry, s, NEG))
            return parts

        def sweep(chunks):
            units = [(c, j, diag) for c, diag in chunks for j in range(pair)]
            ahead = 2

            def unit_logits(c, j, diag):
                return diag_logits(c, j) if diag else logits(c, j, False)

            s = [unit_logits(*u) for u in units[:ahead]]
            for n, (c, j, diag) in enumerate(units):
                if n + ahead < len(units):
                    s.append(unit_logits(*units[n + ahead]))
                if diag:
                    for part, lo in zip(s[n], range(0, CK, TOK)):
                        p = jnp.exp2(part - bound).astype(BF16)
                        acc_ref[j, :, lo:] = acc_ref[j, :, lo:] + _dot(vt_ref[0, j, c, :, lo:lo + TOK], p)
                else:
                    p = jnp.exp2(s[n] - bound).astype(BF16)
                    acc_ref[j] = acc_ref[j] + _dot(vt_ref[0, j, c], p)

        rem = n_full % ATT_TRIP

        def body(i, _):
            sweep([(ATT_TRIP * i + d, False) for d in range(ATT_TRIP)])
            return 0

        lax.fori_loop(0, n_full // ATT_TRIP, body, 0)

        for r in range(ATT_TRIP):
            @pl.when(rem == r)
            def _(r=r):
                sweep([(n_full - r + d, False) for d in range(r)] + [(n_full, True)])

        finalize()

    def scores(c, slot, masked=False):
        for j in range(pair):
            s = logits(c, j, masked)
            s_ref[slot, j] = s
            cm_ref[slot, j] = jnp.max(s, axis=0, keepdims=True)

    def accumulate(c, slot):
        for j in range(pair):
            m = m_ref[j]
            m_new = jnp.maximum(m, cm_ref[slot, j])
            p = jnp.exp2(s_ref[slot, j] - m_new).astype(BF16)
            acc_ref[j] = acc_ref[j] * jnp.exp2(m - m_new) + _dot(vt_ref[0, j, c], p)
            m_ref[j] = m_new

    def online():
        m_ref[...] = jnp.full(m_ref.shape, NEG, F32)
        acc_ref[...] = jnp.zeros(acc_ref.shape, F32)

        steps = n_full - 1
        odd = steps % 2

        @pl.when(n_full == 0)
        def _():
            scores(0, 1, masked=True)

        @pl.when(jnp.logical_and(n_full > 0, odd == 0))
        def _():
            scores(0, 0)

        @pl.when(jnp.logical_and(n_full > 0, odd == 1))
        def _():
            scores(0, 1)
            scores(1, 0)
            accumulate(0, 1)

        def body(i, _):
            c = odd + 2 * i
            scores(c + 1, 1)
            accumulate(c, 0)
            scores(c + 2, 0)
            accumulate(c + 1, 1)
            return 0

        lax.fori_loop(0, steps // 2, body, 0)

        @pl.when(n_full > 0)
        def _():
            scores(n_full, 1, masked=True)
            accumulate(n_full - 1, 0)

        accumulate(n_full, 1)
        finalize()

    bounded()

    @pl.when(2.0 * bound > ATT_SAFE_GAP)
    def _():
        online()


def _fox_attn(qt, k, vt, gt, q_gain, k_gain, hd):
    B, heads, nQ = qt.shape[:3]
    TQ = qt.shape[-1]
    nC, CK = k.shape[2], k.shape[3]
    D, Lp = gt.shape[1], gt.shape[2]
    pair = LANE // hd
    bound = jnp.max(jnp.abs(q_gain)) * jnp.max(jnp.abs(k_gain)) * (hd ** 0.5 * LOG2E)
    return pl.pallas_call(
        functools.partial(_fox_attn_kernel, hd=hd),
        grid=(B, heads // pair, nQ),
        in_specs=[
            pl.BlockSpec(memory_space=pltpu.SMEM),
            pl.BlockSpec((1, pair, 1, LANE, TQ), lambda b, p, q: (b, p, q, 0, 0)),
            pl.BlockSpec((1, pair, nC, CK, LANE), lambda b, p, q: (b, p, 0, 0, 0)),
            pl.BlockSpec((1, pair, nC, vt.shape[3], CK), lambda b, p, q: (b, p, 0, 0, 0)),
            pl.BlockSpec((1, LANE, TQ), lambda b, p, q: (b, p, q)),
        ],
        out_specs=pl.BlockSpec((1, TQ, LANE), lambda b, p, q: (b, q, p)),
        out_shape=jax.ShapeDtypeStruct((B, Lp, D), BF16),
        scratch_shapes=[
            pltpu.VMEM((2, pair, CK, TQ), F32),
            pltpu.VMEM((2, pair, 1, TQ), F32),
            pltpu.VMEM((pair, 1, TQ), F32),
            pltpu.VMEM((pair, vt.shape[3], TQ), F32),
        ],
        compiler_params=_cparams("arbitrary", "arbitrary", "arbitrary"),
        name="fox_attn",
    )(bound.reshape(1).astype(F32), qt, k, vt, gt)


def _ffn_kernel(h_ref, x_ref, wo_ref, g_ref, wg_ref, wu_ref, cw_ref, wd_ref, o_ref, carry_ref):
    @pl.when(pl.program_id(1) == 0)
    def _():
        carry_ref[...] = jnp.zeros_like(carry_ref)

    T = h_ref.shape[1]
    F = wg_ref.shape[1]
    S = FFN_SUB
    n_chunks = F // FFN_CHUNK
    row = lax.broadcasted_iota(jnp.int32, (S, FFN_CHUNK), 0)
    h1s = [h_ref[0, r * S:(r + 1) * S] + _dot(x_ref[0, r * S:(r + 1) * S], wo_ref[...]) for r in range(T // S)]
    xns = [_rms(h1, g_ref[...]).astype(BF16) for h1 in h1s]
    tails = [carry_ref[:, c * FFN_CHUNK:(c + 1) * FFN_CHUNK] for c in range(n_chunks)]

    for r, xn in enumerate(xns):
        def gate_up(c):
            sl = slice(c * FFN_CHUNK, (c + 1) * FFN_CHUNK)
            return _dot(xn, wg_ref[:, sl]), _dot(xn, wu_ref[:, sl])

        acc = h1s[r]
        nxt = gate_up(0)
        for c in range(n_chunks):
            sl = slice(c * FFN_CHUNK, (c + 1) * FFN_CHUNK)
            a, u = nxt
            if c + 1 < n_chunks:
                nxt = gate_up(c + 1)
            prev = tails[c]
            a1 = jnp.where(row == 0, prev[SUBLANE - 1:SUBLANE], pltpu.roll(a, 1, 0))
            a2 = jnp.where(row == 0, prev[SUBLANE - 2:SUBLANE - 1],
                           jnp.where(row == 1, prev[SUBLANE - 1:SUBLANE], pltpu.roll(a, 2, 0)))
            tails[c] = a[S - SUBLANE:S]
            cw = cw_ref[:, sl]
            z = cw[3:4] + cw[0:1] * a2 + cw[1:2] * a1 + cw[2:3] * a
            act = (z * jax.nn.sigmoid(z) * u).astype(BF16)
            acc = acc + _dot(act, wd_ref[sl, :])
        o_ref[0, r * S:(r + 1) * S] = acc

    for c in range(n_chunks):
        carry_ref[:, c * FFN_CHUNK:(c + 1) * FFN_CHUNK] = tails[c]


def _ffn(h, x, w_out, gain, w_gate, w_up, conv_w, conv_b, w_down):
    B, Lp, D = h.shape
    F = w_gate.shape[1]
    cw = jnp.concatenate([conv_w, conv_b[None], jnp.zeros((SUBLANE - conv_w.shape[0] - 1, F), F32)], axis=0)
    return pl.pallas_call(
        _ffn_kernel,
        grid=(B, Lp // FFN_ROWS),
        in_specs=[
            pl.BlockSpec((1, FFN_ROWS, D), lambda b, t: (b, t, 0)),
            pl.BlockSpec((1, FFN_ROWS, D), lambda b, t: (b, t, 0)),
            _resident((D, D)), _resident((1, D)), _resident((D, F)), _resident((D, F)),
            _resident((SUBLANE, F)), _resident((F, D)),
        ],
        out_specs=pl.BlockSpec((1, FFN_ROWS, D), lambda b, t: (b, t, 0)),
        out_shape=jax.ShapeDtypeStruct((B, Lp, D), F32),
        scratch_shapes=[pltpu.VMEM((SUBLANE, F), F32)],
        compiler_params=_cparams("arbitrary", "arbitrary"),
        name="ffn",
    )(h, x, w_out.astype(BF16), gain.reshape(1, D), w_gate.astype(BF16), w_up.astype(BF16), cw,
      w_down.astype(BF16))


def _hgrn_in_kernel(h_ref, g_ref, w_ref, lbs_ref, q_ref, lf_ref, v_ref, sg_ref, *, layer):
    R, D = h_ref.shape
    T = TOK
    lbs = lbs_ref[...]
    e = jnp.exp(lbs - jnp.max(lbs, axis=0, keepdims=True))
    p = e / jnp.sum(e, axis=0, keepdims=True)
    lb = jnp.sum(p[1:layer + 1], axis=0, keepdims=True) if layer > 0 else jnp.zeros((1, D), F32)

    xns = [_rms(h_ref[r * T:(r + 1) * T], g_ref[...]).astype(BF16) for r in range(R // T)]
    for r, xn in enumerate(xns):
        rows = slice(r * T, (r + 1) * T)

        def proj(i):
            return _dot(xn, w_ref[:, i * D:(i + 1) * D])

        yq = proj(0)
        q_ref[rows] = (yq * jax.nn.sigmoid(yq)).astype(q_ref.dtype)
        f = lb + (1.0 - lb) * jax.nn.sigmoid(proj(1))
        lf_ref[rows] = jnp.log(f)
        v_ref[rows] = proj(2).astype(v_ref.dtype)
        sg_ref[rows] = jax.nn.sigmoid(proj(3)).astype(sg_ref.dtype)


def _hgrn_in(h, gain, w, lower_bounds, layer):
    B, Lp, D = h.shape
    rows = B * Lp
    blk = pl.BlockSpec((HGRN_IN_ROWS, D), lambda i: (i, 0))
    outs = pl.pallas_call(
        functools.partial(_hgrn_in_kernel, layer=layer),
        grid=(rows // HGRN_IN_ROWS,),
        in_specs=[blk, _resident((1, D)), _resident(w.shape), _resident(lower_bounds.shape)],
        out_specs=[blk, blk, blk, blk],
        out_shape=[
            jax.ShapeDtypeStruct((rows, D), BF16),
            jax.ShapeDtypeStruct((rows, D), F32),
            jax.ShapeDtypeStruct((rows, D), BF16),
            jax.ShapeDtypeStruct((rows, D), BF16),
        ],
        compiler_params=_cparams("arbitrary"),
        name="hgrn_in",
    )(h.reshape(rows, D), gain.reshape(1, D), w, lower_bounds)
    return [o.reshape(B, Lp, D) for o in outs]


def _hgrn_rec_kernel(q_ref, lf_ref, v_ref, sg_ref, og_ref, o_ref, st_ref, stn_ref, *, dk):
    @pl.when(pl.program_id(1) == 0)
    def _():
        st_ref[...] = jnp.zeros_like(st_ref)

    C = HGRN_CHUNK
    R, D = q_ref.shape[1], q_ref.shape[2]
    sub = HGRN_SUB
    nsub = C // sub
    r_cc = lax.broadcasted_iota(jnp.int32, (C, C), 0)
    c_cc = lax.broadcasted_iota(jnp.int32, (C, C), 1)
    causal = c_cc <= r_cc
    ltri = jnp.where(causal, 1.0, 0.0).astype(BF16)
    row = lax.broadcasted_iota(jnp.int32, (C, dk), 0)

    def run(factored_diag, row0, nrows, src_ref):
        if not factored_diag:
            diag_mask = jnp.logical_and(c_cc >= r_cc - (r_cc & (sub - 1)), causal)
            d_sel = (lax.broadcasted_iota(jnp.int32, (sub * dk, C), 0)
                     - (lax.broadcasted_iota(jnp.int32, (sub * dk, C), 1) & (sub - 1)) * dk)
            sel = jnp.where(jnp.logical_and(d_sel >= 0, d_sel < dk), 1.0, 0.0).astype(BF16)
        n_keys = [(i + 1) * sub if factored_diag else i * sub for i in range(nsub)]

        units = [(hh, ci) for hh in range(D // dk) for ci in range(nrows // C)]

        def rows_of(ci):
            return pl.ds(row0 + ci * C, C)

        def blk(ref, u):
            hh, ci = u
            return ref[0, rows_of(ci), hh * dk:(hh + 1) * dk]

        b = {}
        for u in units:
            hi, mid, lo = _split3(blk(lf_ref, u))
            b[u] = (_dot(ltri, hi) + _dot(ltri, mid) + _dot(ltri, lo)) * LOG2E

        qe, kk, strips = {}, {}, {}
        for u in units:
            q = blk(q_ref, u).astype(F32)
            kk[u] = 1.0 - jnp.exp(blk(lf_ref, u))
            qe[u] = (q * jnp.exp2(b[u])).astype(BF16)
            parts = []
            for i in range(nsub):
                if n_keys[i] == 0:
                    parts.append(jnp.zeros((sub, C), F32))
                    continue
                bref = b[u][i * sub - 1:i * sub] if i > 0 else jnp.zeros((1, dk), F32)
                qi = q[i * sub:(i + 1) * sub] * jnp.exp2(b[u][i * sub:(i + 1) * sub] - bref)
                ke = kk[u] * jnp.exp2(jnp.where(row < n_keys[i], bref - b[u], NEG))
                parts.append(_dot_nt(qi.astype(BF16), ke.astype(BF16)))
            strips[u] = parts

        intra, kv = {}, {}
        for u in units:
            a = jnp.concatenate(strips[u], axis=0)
            if factored_diag:
                a = jnp.where(causal, a, 0.0)
            else:
                b4 = b[u].reshape(nsub, sub, dk)
                q4 = blk(q_ref, u).astype(F32).reshape(nsub, sub, dk)
                k4 = kk[u].reshape(nsub, sub, dk)
                z = [(q4 * jnp.exp2(jnp.minimum(b4 - b4[:, s:s + 1], 0.0)) * k4[:, s:s + 1])
                     .reshape(C, dk).astype(BF16) for s in range(sub)]
                a = jnp.where(diag_mask, _dot(jnp.concatenate(z, axis=1), sel), a)
            v = blk(v_ref, u)
            intra[u] = _dot(a.astype(BF16), v)
            kv[u] = _dot_tn(v, (kk[u] * jnp.exp2(b[u][C - 1:C] - b[u])).astype(BF16))

        out = {}
        for hh in range(D // dk):
            st = src_ref[hh]
            for ci in range(nrows // C):
                u = (hh, ci)
                out[u] = _dot_nt(qe[u], st.astype(BF16)) + intra[u]
                st = st * jnp.exp2(b[u][C - 1:C]) + kv[u]
            stn_ref[hh] = st

        for u in units:
            hh, ci = u
            on = _rms(out[u], og_ref[...]) * blk(sg_ref, u).astype(F32)
            o_ref[0, rows_of(ci), hh * dk:(hh + 1) * dk] = on.astype(o_ref.dtype)

    worst = -jnp.min(jnp.sum(lf_ref[0].reshape(R // sub, sub, D), axis=1))
    run(True, 0, R, st_ref)

    @pl.when(worst > HGRN_SAFE_DECAY)
    def _():
        stn_ref[...] = st_ref[...]

        def piece(i, _):
            run(False, pl.multiple_of(i * HGRN_RERUN_ROWS, HGRN_RERUN_ROWS), HGRN_RERUN_ROWS, stn_ref)
            return 0

        lax.fori_loop(0, R // HGRN_RERUN_ROWS, piece, 0)

    st_ref[...] = stn_ref[...]


def _hgrn_rec(q, lf, v, sg, o_gain):
    B, Lp, D = q.shape
    dk = o_gain.shape[0]
    blk = pl.BlockSpec((1, HGRN_ROWS, D), lambda b, t: (b, t, 0))
    return pl.pallas_call(
        functools.partial(_hgrn_rec_kernel, dk=dk),
        grid=(B, Lp // HGRN_ROWS),
        in_specs=[blk, blk, blk, blk, _resident((1, dk))],
        out_specs=blk,
        out_shape=jax.ShapeDtypeStruct((B, Lp, D), BF16),
        scratch_shapes=[pltpu.VMEM((D // dk, dk, dk), F32)] * 2,
        compiler_params=_cparams("arbitrary", "arbitrary"),
        name="hgrn_rec",
    )(q, lf, v, sg, o_gain.reshape(1, dk))


def kernel(x, meta_tokens, fox_norm, fox_w_in, fox_b_f, fox_q_gain, fox_k_gain, fox_w_out,
           hgrn_norm, hgrn_w_in, hgrn_lower_bounds, hgrn_o_gain, hgrn_w_out,
           ffn_norm, ffn_w_gate, ffn_w_up, ffn_conv_w, ffn_conv_b, ffn_w_down):
    B, S, D = x.shape
    L = S + N_META
    Lp = -(-L // TOK) * TOK
    depth = ffn_norm.shape[0]
    heads = fox_b_f.shape[1]
    hd = D // heads
    assert LANE % hd == 0 and hd + N_EXTRA <= LANE and D % LANE == 0
    assert Lp % FOX_ROWS == 0 and (B * Lp) % HGRN_IN_ROWS == 0 and HGRN_IN_ROWS % TOK == 0
    assert Lp % FFN_ROWS == 0 and FFN_ROWS % FFN_SUB == 0 and Lp % HGRN_ROWS == 0 and HGRN_ROWS % HGRN_RERUN_ROWS == 0 and HGRN_RERUN_ROWS % HGRN_CHUNK == 0 and (Lp // TOK) % KEY_BLOCKS == 0 and (Lp // TOK) % Q_BLOCKS == 0 and KEY_BLOCKS % Q_BLOCKS == 0

    meta = jnp.broadcast_to(meta_tokens[None].astype(x.dtype), (B, N_META, D))
    h = jnp.concatenate([meta, x, jnp.zeros((B, Lp - L, D), x.dtype)], axis=1)

    for layer in range(depth):
        j = layer // 2
        if layer % 2 == 0:
            w = fox_w_in[j]
            w_t = jnp.concatenate([w[:, :3 * D], w[:, 3 * D + heads:]], axis=1).T.astype(BF16)
            wf_t = w[:, 3 * D:3 * D + heads].T.astype(BF16)
            qt, k, vt, gt = _fox_in(h, fox_norm[j], w_t, wf_t, fox_b_f[j],
                                    fox_q_gain[j], fox_k_gain[j], heads)
            k = k.reshape(B, heads, -1, KEY_BLOCKS * TOK, LANE)
            og, w_out = _fox_attn(qt, k, vt, gt, fox_q_gain[j], fox_k_gain[j], hd), fox_w_out[j]
        else:
            q, lf, v, sg = _hgrn_in(h, hgrn_norm[j], hgrn_w_in[j].astype(BF16),
                                    hgrn_lower_bounds, layer)
            og, w_out = _hgrn_rec(q, lf, v, sg, hgrn_o_gain[j]), hgrn_w_out[j]
        h = _ffn(h, og, w_out, ffn_norm[layer], ffn_w_gate[layer], ffn_w_up[layer],
                 ffn_conv_w[layer], ffn_conv_b[layer], ffn_w_down[layer])
    return h[:, N_META:L]
```

```python
import functools

import jax
import jax.numpy as jnp
from jax import lax
from jax.experimental import pallas as pl
from jax.experimental.pallas import tpu as pltpu

N_META = 16
EPS = 1e-6
NEG = -1e30
LANE = 128
SUBLANE = 8
TOK = 256
KEY_BLOCKS = 3
Q_BLOCKS = 3
FOX_ROWS = 768
FFN_ROWS = 768
FFN_SUB = 256
FFN_CHUNK = 256
HGRN_IN_ROWS = 768
HGRN_CHUNK = 64
HGRN_SUB = 16
HGRN_ROWS = 256
HGRN_RERUN_ROWS = 128
HGRN_SAFE_DECAY = 60.0
LOG2E = 1.4426950408889634
N_EXTRA = 16
ATT_TRIP = 4
ATT_SAFE_GAP = 96.0
VMEM_LIMIT = 56 * 1024 * 1024

F32 = jnp.float32
BF16 = jnp.bfloat16


def _cparams(*sem):
    return pltpu.CompilerParams(dimension_semantics=sem, vmem_limit_bytes=VMEM_LIMIT)


def _resident(shape):
    zeros = (0,) * len(shape)
    return pl.BlockSpec(shape, lambda *_: zeros, pipeline_mode=pl.Buffered(1))


def _split3(x):
    hi = x.astype(BF16)
    r = x - hi.astype(F32)
    mid = r.astype(BF16)
    lo = (r - mid.astype(F32)).astype(BF16)
    return hi, mid, lo


def _dot(a, b):
    return jnp.dot(a, b, preferred_element_type=F32)


def _dot_nt(a, b):
    return lax.dot_general(a, b, (((1,), (1,)), ((), ())), preferred_element_type=F32)


def _dot_tn(a, b):
    return lax.dot_general(a, b, (((0,), (0,)), ((), ())), preferred_element_type=F32)


def _rms(x, gain):
    ms = jnp.mean(x * x, axis=-1, keepdims=True)
    return x * lax.rsqrt(ms + EPS) * gain


def _fox_in_kernel(h_ref, g_ref, w_ref, bf_ref, qg_ref, kg_ref,
                   qt_ref, k_ref, vt_ref, gt_ref, carry_ref, *, heads):
    @pl.when(pl.program_id(1) == 0)
    def _():
        carry_ref[...] = jnp.zeros_like(carry_ref)

    R, D = h_ref.shape[1], h_ref.shape[2]
    T = TOK
    hd = D // heads
    xnts = [_rms(h_ref[0, r * T:(r + 1) * T], g_ref[...]).T.astype(BF16) for r in range(R // T)]

    src = lax.broadcasted_iota(jnp.int32, (T, T), 0)
    dst = lax.broadcasted_iota(jnp.int32, (T, T), 1)
    upper = jnp.where(src <= dst, 1.0, 0.0).astype(BF16)
    row = lax.broadcasted_iota(jnp.int32, (N_EXTRA, T), 0)
    zpad = jnp.zeros((LANE - hd - N_EXTRA, T), F32)
    ones_row = jnp.where(row == 0, 1.0, 0.0)
    carry = carry_ref[:, 0:1]

    def head_norm(y, gain_ref, scale):
        y = y.reshape(heads, hd, T)
        ms = jnp.mean(y * y, axis=1, keepdims=True)
        return y * lax.rsqrt(ms + EPS) * (gain_ref[...] * scale)

    def project(xnt):
        f0 = 3 * D
        return ([_dot(w_ref[i * D:(i + 1) * D, :], xnt) for i in range(3)]
                + [_dot(w_ref[f0:f0 + heads, :], xnt), _dot(w_ref[f0 + heads:f0 + heads + D, :], xnt)])

    nxt = project(xnts[0])
    for r in range(len(xnts)):
        cols = slice(r * T, (r + 1) * T)
        yq, yk, yv, fl, yg = nxt
        if r + 1 < len(xnts):
            nxt = project(xnts[r + 1])

        fl = fl + bf_ref[...]
        lf = jnp.minimum(fl, 0.0) - jnp.log(1.0 + jnp.exp(-jnp.abs(fl)))
        hi, mid, lo = _split3(lf)
        c = carry + (_dot(hi, upper) + _dot(mid, upper) + _dot(lo, upper))
        carry = carry + jnp.sum(lf, axis=1, keepdims=True)
        chi, cmid, clo = (p.astype(F32) for p in _split3(c * LOG2E))

        qn = head_norm(yq, qg_ref, hd ** -0.5 * LOG2E)
        kn = head_norm(yk, kg_ref, 1.0)
        vv = yv.reshape(heads, hd, T)
        gt_ref[0, :, cols] = jax.nn.sigmoid(yg).astype(gt_ref.dtype)

        for hh in range(heads):
            a, b, d = (jnp.broadcast_to(p[hh:hh + 1], (N_EXTRA, T)) for p in (chi, cmid, clo))
            eq = jnp.where(row == 0, a, jnp.where(row == 1, b, jnp.where(row == 2, d, jnp.where(row < 6, 1.0, 0.0))))
            ek = jnp.where(row < 3, 1.0, jnp.where(row == 3, -a, jnp.where(row == 4, -b, jnp.where(row == 5, -d, 0.0))))
            qt_ref[0, hh, 0, :, cols] = jnp.concatenate([qn[hh], eq, zpad], axis=0).astype(qt_ref.dtype)
            k_ref[0, hh, r] = jnp.concatenate([kn[hh], ek, zpad], axis=0).T.astype(k_ref.dtype)
            vt_ref[0, hh, 0, :, cols] = jnp.concatenate([vv[hh], ones_row], axis=0).astype(vt_ref.dtype)

    carry_ref[...] = jnp.broadcast_to(carry, carry_ref.shape)


def _fox_in(h, gain, w_t, b_f, q_gain, k_gain, heads):
    B, Lp, D = h.shape
    hd = D // heads
    R = FOX_ROWS
    nR, sub = Lp // R, R // TOK
    assert R == Q_BLOCKS * TOK == KEY_BLOCKS * TOK
    return pl.pallas_call(
        functools.partial(_fox_in_kernel, heads=heads),
        grid=(B, nR),
        in_specs=[
            pl.BlockSpec((1, R, D), lambda b, t: (b, t, 0)),
            _resident((1, D)), _resident(w_t.shape),
            _resident((heads, 1)), _resident((hd, 1)), _resident((hd, 1)),
        ],
        out_specs=[
            pl.BlockSpec((1, heads, 1, LANE, R), lambda b, t: (b, 0, t, 0, 0)),
            pl.BlockSpec((1, heads, sub, TOK, LANE), lambda b, t: (b, 0, t, 0, 0)),
            pl.BlockSpec((1, heads, 1, hd + N_EXTRA, R), lambda b, t: (b, 0, t, 0, 0)),
            pl.BlockSpec((1, D, R), lambda b, t: (b, 0, t)),
        ],
        out_shape=[
            jax.ShapeDtypeStruct((B, heads, nR, LANE, R), BF16),
            jax.ShapeDtypeStruct((B, heads, nR * sub, TOK, LANE), BF16),
            jax.ShapeDtypeStruct((B, heads, nR, hd + N_EXTRA, R), BF16),
            jax.ShapeDtypeStruct((B, D, Lp), BF16),
        ],
        scratch_shapes=[pltpu.VMEM((heads, LANE), F32)],
        compiler_params=_cparams("arbitrary", "arbitrary"),
        name="fox_in",
    )(h, gain.reshape(1, D), w_t, b_f.reshape(heads, 1),
      q_gain.reshape(hd, 1), k_gain.reshape(hd, 1))


def _fox_attn_kernel(bound_ref, qt_ref, k_ref, vt_ref, gt_ref, o_ref, s_ref, cm_ref, m_ref, acc_ref, *, hd):
    qi = pl.program_id(2)
    T = qt_ref.shape[-1]
    pair = qt_ref.shape[1]
    CK = k_ref.shape[3]
    n_full = (qi * T) // CK
    assert CK == T
    bound = bound_ref[0]

    def logits(c, j, masked):
        s = _dot(k_ref[0, j, c], qt_ref[0, j, 0])
        if masked:
            key = c * CK - qi * T + lax.broadcasted_iota(jnp.int32, (CK, T), 0)
            qry = lax.broadcasted_iota(jnp.int32, (CK, T), 1)
            s = jnp.where(key <= qry, s, NEG)
        return s

    def finalize():
        o = jnp.concatenate([acc_ref[j, :hd] / acc_ref[j, hd:hd + 1] for j in range(pair)], axis=0)
        o_ref[0] = (o * gt_ref[0].astype(F32)).T.astype(o_ref.dtype)

    def bounded():
        acc_ref[...] = jnp.zeros(acc_ref.shape, F32)

        def diag_logits(c, j):
            parts = []
            for lo in range(0, CK, TOK):
                s = _dot(k_ref[0, j, c, lo:lo + TOK, :], qt_ref[0, j, 0, :, lo:])
                key = lax.broadcasted_iota(jnp.int32, s.shape, 0)
                qry = lax.broadcasted_iota(jnp.int32, s.shape, 1)
                parts.append(jnp.where(key <= qry, s, NEG))
            return parts

        def sweep(chunks):
            units = [(c, j, diag) for c, diag in chunks for j in range(pair)]
            ahead = 2

            def unit_logits(c, j, diag):
                return diag_logits(c, j) if diag else logits(c, j, False)

            s = [unit_logits(*u) for u in units[:ahead]]
            for n, (c, j, diag) in enumerate(units):
                if n + ahead < len(units):
                    s.append(unit_logits(*units[n + ahead]))
                if diag:
                    for part, lo in zip(s[n], range(0, CK, TOK)):
                        p = jnp.exp2(part - bound).astype(BF16)
                        acc_ref[j, :, lo:] = acc_ref[j, :, lo:] + _dot(vt_ref[0, j, c, :, lo:lo + TOK], p)
                else:
                    p = jnp.exp2(s[n] - bound).astype(BF16)
                    acc_ref[j] = acc_ref[j] + _dot(vt_ref[0, j, c], p)

        rem = n_full % ATT_TRIP

        def body(i, _):
            sweep([(ATT_TRIP * i + d, False) for d in range(ATT_TRIP)])
            return 0

        lax.fori_loop(0, n_full // ATT_TRIP, body, 0)

        for r in range(ATT_TRIP):
            @pl.when(rem == r)
            def _(r=r):
                sweep([(n_full - r + d, False) for d in range(r)] + [(n_full, True)])

        finalize()

    def scores(c, slot, masked=False):
        for j in range(pair):
            s = logits(c, j, masked)
            s_ref[slot, j] = s
            cm_ref[slot, j] = jnp.max(s, axis=0, keepdims=True)

    def accumulate(c, slot):
        for j in range(pair):
            m = m_ref[j]
            m_new = jnp.maximum(m, cm_ref[slot, j])
            p = jnp.exp2(s_ref[slot, j] - m_new).astype(BF16)
            acc_ref[j] = acc_ref[j] * jnp.exp2(m - m_new) + _dot(vt_ref[0, j, c], p)
            m_ref[j] = m_new

    def online():
        m_ref[...] = jnp.full(m_ref.shape, NEG, F32)
        acc_ref[...] = jnp.zeros(acc_ref.shape, F32)

        steps = n_full - 1
        odd = steps % 2

        @pl.when(n_full == 0)
        def _():
            scores(0, 1, masked=True)

        @pl.when(jnp.logical_and(n_full > 0, odd == 0))
        def _():
            scores(0, 0)

        @pl.when(jnp.logical_and(n_full > 0, odd == 1))
        def _():
            scores(0, 1)
            scores(1, 0)
            accumulate(0, 1)

        def body(i, _):
            c = odd + 2 * i
            scores(c + 1, 1)
            accumulate(c, 0)
            scores(c + 2, 0)
            accumulate(c + 1, 1)
            return 0

        lax.fori_loop(0, steps // 2, body, 0)

        @pl.when(n_full > 0)
        def _():
            scores(n_full, 1, masked=True)
            accumulate(n_full - 1, 0)

        accumulate(n_full, 1)
        finalize()

    bounded()

    @pl.when(2.0 * bound > ATT_SAFE_GAP)
    def _():
        online()


def _fox_attn(qt, k, vt, gt, q_gain, k_gain, hd):
    B, heads, nQ = qt.shape[:3]
    TQ = qt.shape[-1]
    nC, CK = k.shape[2], k.shape[3]
    D, Lp = gt.shape[1], gt.shape[2]
    pair = LANE // hd
    bound = jnp.max(jnp.abs(q_gain)) * jnp.max(jnp.abs(k_gain)) * (hd ** 0.5 * LOG2E)
    return pl.pallas_call(
        functools.partial(_fox_attn_kernel, hd=hd),
        grid=(B, heads // pair, nQ),
        in_specs=[
            pl.BlockSpec(memory_space=pltpu.SMEM),
            pl.BlockSpec((1, pair, 1, LANE, TQ), lambda b, p, q: (b, p, q, 0, 0)),
            pl.BlockSpec((1, pair, nC, CK, LANE), lambda b, p, q: (b, p, 0, 0, 0)),
            pl.BlockSpec((1, pair, nC, vt.shape[3], CK), lambda b, p, q: (b, p, 0, 0, 0)),
            pl.BlockSpec((1, LANE, TQ), lambda b, p, q: (b, p, q)),
        ],
        out_specs=pl.BlockSpec((1, TQ, LANE), lambda b, p, q: (b, q, p)),
        out_shape=jax.ShapeDtypeStruct((B, Lp, D), BF16),
        scratch_shapes=[
            pltpu.VMEM((2, pair, CK, TQ), F32),
            pltpu.VMEM((2, pair, 1, TQ), F32),
            pltpu.VMEM((pair, 1, TQ), F32),
            pltpu.VMEM((pair, vt.shape[3], TQ), F32),
        ],
        compiler_params=_cparams("arbitrary", "arbitrary", "arbitrary"),
        name="fox_attn",
    )(bound.reshape(1).astype(F32), qt, k, vt, gt)


def _ffn_kernel(h_ref, x_ref, wo_ref, g_ref, wg_ref, wu_ref, cw_ref, wd_ref, o_ref, carry_ref, *stage,
                skip_rows=0, keep_rows=0, n_b=0, n_t=0):
    @pl.when(pl.program_id(1) == 0)
    def _():
        carry_ref[...] = jnp.zeros_like(carry_ref)

    T = h_ref.shape[1]
    F = wg_ref.shape[1]
    S = FFN_SUB
    n_chunks = F // FFN_CHUNK
    row = lax.broadcasted_iota(jnp.int32, (S, FFN_CHUNK), 0)

    if stage:
        buf_ref, sem_ref = stage
        n_steps = n_b * n_t
        step = pl.program_id(0) * n_t + pl.program_id(1)
        slot = step % 2
        last_rows = skip_rows + keep_rows - (n_t - 1) * T

        def copy_out(st, start):
            b, t = st // n_t, st % n_t
            sl = st % 2

            def go(src, dst):
                cp = pltpu.make_async_copy(buf_ref.at[sl, src], o_ref.at[b, dst], sem_ref.at[sl])
                cp.start() if start else cp.wait()

            @pl.when(t == 0)
            def _():
                go(pl.ds(skip_rows, T - skip_rows), pl.ds(0, T - skip_rows))

            @pl.when(jnp.logical_and(t > 0, t < n_t - 1))
            def _():
                go(pl.ds(0, T), pl.ds(pl.multiple_of(t * T - skip_rows, SUBLANE), T))

            @pl.when(jnp.logical_and(t == n_t - 1, t > 0))
            def _():
                go(pl.ds(0, last_rows), pl.ds(keep_rows - last_rows, last_rows))

        @pl.when(step >= 2)
        def _():
            copy_out(step - 2, start=False)

    h1s = [h_ref[0, r * S:(r + 1) * S] + _dot(x_ref[0, r * S:(r + 1) * S], wo_ref[...]) for r in range(T // S)]
    xns = [_rms(h1, g_ref[...]).astype(BF16) for h1 in h1s]
    tails = [carry_ref[:, c * FFN_CHUNK:(c + 1) * FFN_CHUNK] for c in range(n_chunks)]

    for r, xn in enumerate(xns):
        def gate_up(c):
            sl = slice(c * FFN_CHUNK, (c + 1) * FFN_CHUNK)
            return _dot(xn, wg_ref[:, sl]), _dot(xn, wu_ref[:, sl])

        acc = h1s[r]
        nxt = gate_up(0)
        for c in range(n_chunks):
            sl = slice(c * FFN_CHUNK, (c + 1) * FFN_CHUNK)
            a, u = nxt
            if c + 1 < n_chunks:
                nxt = gate_up(c + 1)
            prev = tails[c]
            a1 = jnp.where(row == 0, prev[SUBLANE - 1:SUBLANE], pltpu.roll(a, 1, 0))
            a2 = jnp.where(row == 0, prev[SUBLANE - 2:SUBLANE - 1],
                           jnp.where(row == 1, prev[SUBLANE - 1:SUBLANE], pltpu.roll(a, 2, 0)))
            tails[c] = a[S - SUBLANE:S]
            cw = cw_ref[:, sl]
            z = cw[3:4] + cw[0:1] * a2 + cw[1:2] * a1 + cw[2:3] * a
            act = (z * jax.nn.sigmoid(z) * u).astype(BF16)
            acc = acc + _dot(act, wd_ref[sl, :])
        if stage:
            stage[0][slot, r * S:(r + 1) * S] = acc
        else:
            o_ref[0, r * S:(r + 1) * S] = acc

    for c in range(n_chunks):
        carry_ref[:, c * FFN_CHUNK:(c + 1) * FFN_CHUNK] = tails[c]

    if stage:
        copy_out(step, start=True)

        @pl.when(step == n_steps - 1)
        def _():
            if n_steps > 1:
                copy_out(step - 1, start=False)
            copy_out(step, start=False)


def _ffn(h, x, w_out, gain, w_gate, w_up, conv_w, conv_b, w_down, keep=None):
    B, Lp, D = h.shape
    F = w_gate.shape[1]
    cw = jnp.concatenate([conv_w, conv_b[None], jnp.zeros((SUBLANE - conv_w.shape[0] - 1, F), F32)], axis=0)
    if keep is None:
        kern = _ffn_kernel
        out_spec = pl.BlockSpec((1, FFN_ROWS, D), lambda b, t: (b, t, 0))
        out_shape = jax.ShapeDtypeStruct((B, Lp, D), F32)
        staging = []
    else:
        skip_rows, keep_rows = keep
        assert skip_rows % SUBLANE == 0 and 0 < skip_rows < FFN_ROWS and Lp // FFN_ROWS >= 2
        assert (Lp // FFN_ROWS - 1) * FFN_ROWS < skip_rows + keep_rows <= Lp
        kern = functools.partial(_ffn_kernel, skip_rows=skip_rows, keep_rows=keep_rows, n_b=B, n_t=Lp // FFN_ROWS)
        out_spec = pl.BlockSpec(memory_space=pl.ANY)
        out_shape = jax.ShapeDtypeStruct((B, keep_rows, D), F32)
        staging = [pltpu.VMEM((2, FFN_ROWS, D), F32), pltpu.SemaphoreType.DMA((2,))]
    return pl.pallas_call(
        kern,
        grid=(B, Lp // FFN_ROWS),
        in_specs=[
            pl.BlockSpec((1, FFN_ROWS, D), lambda b, t: (b, t, 0)),
            pl.BlockSpec((1, FFN_ROWS, D), lambda b, t: (b, t, 0)),
            _resident((D, D)), _resident((1, D)), _resident((D, F)), _resident((D, F)),
            _resident((SUBLANE, F)), _resident((F, D)),
        ],
        out_specs=out_spec,
        out_shape=out_shape,
        scratch_shapes=[pltpu.VMEM((SUBLANE, F), F32)] + staging,
        compiler_params=_cparams("arbitrary", "arbitrary"),
        name="ffn",
    )(h, x, w_out.astype(BF16), gain.reshape(1, D), w_gate.astype(BF16), w_up.astype(BF16), cw,
      w_down.astype(BF16))


def _hgrn_in_kernel(h_ref, g_ref, w_ref, lbs_ref, q_ref, lf_ref, v_ref, sg_ref, *, layer):
    R, D = h_ref.shape
    T = TOK
    lbs = lbs_ref[...]
    e = jnp.exp(lbs - jnp.max(lbs, axis=0, keepdims=True))
    p = e / jnp.sum(e, axis=0, keepdims=True)
    lb = jnp.sum(p[1:layer + 1], axis=0, keepdims=True) if layer > 0 else jnp.zeros((1, D), F32)

    xns = [_rms(h_ref[r * T:(r + 1) * T], g_ref[...]).astype(BF16) for r in range(R // T)]
    def project(xn):
        return [_dot(xn, w_ref[:, i * D:(i + 1) * D]) for i in range(4)]

    nxt = project(xns[0])
    for r in range(len(xns)):
        rows = slice(r * T, (r + 1) * T)
        yq, yf, yi, yg = nxt
        if r + 1 < len(xns):
            nxt = project(xns[r + 1])
        q_ref[rows] = (yq * jax.nn.sigmoid(yq)).astype(q_ref.dtype)
        f = lb + (1.0 - lb) * jax.nn.sigmoid(yf)
        lf_ref[rows] = jnp.log(f)
        v_ref[rows] = yi.astype(v_ref.dtype)
        sg_ref[rows] = jax.nn.sigmoid(yg).astype(sg_ref.dtype)


def _hgrn_in(h, gain, w, lower_bounds, layer):
    B, Lp, D = h.shape
    rows = B * Lp
    blk = pl.BlockSpec((HGRN_IN_ROWS, D), lambda i: (i, 0))
    outs = pl.pallas_call(
        functools.partial(_hgrn_in_kernel, layer=layer),
        grid=(rows // HGRN_IN_ROWS,),
        in_specs=[blk, _resident((1, D)), _resident(w.shape), _resident(lower_bounds.shape)],
        out_specs=[blk, blk, blk, blk],
        out_shape=[
            jax.ShapeDtypeStruct((rows, D), BF16),
            jax.ShapeDtypeStruct((rows, D), F32),
            jax.ShapeDtypeStruct((rows, D), BF16),
            jax.ShapeDtypeStruct((rows, D), BF16),
        ],
        compiler_params=_cparams("arbitrary"),
        name="hgrn_in",
    )(h.reshape(rows, D), gain.reshape(1, D), w, lower_bounds)
    return [o.reshape(B, Lp, D) for o in outs]


def _hgrn_rec_kernel(q_ref, lf_ref, v_ref, sg_ref, og_ref, o_ref, st_ref, stn_ref, *, dk):
    @pl.when(pl.program_id(1) == 0)
    def _():
        st_ref[...] = jnp.zeros_like(st_ref)

    C = HGRN_CHUNK
    R, D = q_ref.shape[1], q_ref.shape[2]
    sub = HGRN_SUB
    nsub = C // sub
    r_cc = lax.broadcasted_iota(jnp.int32, (C, C), 0)
    c_cc = lax.broadcasted_iota(jnp.int32, (C, C), 1)
    causal = c_cc <= r_cc
    ltri = jnp.where(causal, 1.0, 0.0).astype(BF16)
    row = lax.broadcasted_iota(jnp.int32, (C, dk), 0)

    def run(factored_diag, row0, nrows, src_ref):
        if not factored_diag:
            diag_mask = jnp.logical_and(c_cc >= r_cc - (r_cc & (sub - 1)), causal)
            d_sel = (lax.broadcasted_iota(jnp.int32, (sub * dk, C), 0)
                     - (lax.broadcasted_iota(jnp.int32, (sub * dk, C), 1) & (sub - 1)) * dk)
            sel = jnp.where(jnp.logical_and(d_sel >= 0, d_sel < dk), 1.0, 0.0).astype(BF16)
        n_keys = [(i + 1) * sub if factored_diag else i * sub for i in range(nsub)]

        units = [(hh, ci) for hh in range(D // dk) for ci in range(nrows // C)]

        def rows_of(ci):
            return pl.ds(row0 + ci * C, C)

        def blk(ref, u):
            hh, ci = u
            return ref[0, rows_of(ci), hh * dk:(hh + 1) * dk]

        b = {}
        for u in units:
            hi, mid, lo = _split3(blk(lf_ref, u))
            b[u] = (_dot(ltri, hi) + _dot(ltri, mid) + _dot(ltri, lo)) * LOG2E

        qe, kk, strips = {}, {}, {}
        for u in units:
            q = blk(q_ref, u).astype(F32)
            kk[u] = 1.0 - jnp.exp(blk(lf_ref, u))
            qe[u] = (q * jnp.exp2(b[u])).astype(BF16)
            parts = []
            for i in range(nsub):
                if n_keys[i] == 0:
                    parts.append(jnp.zeros((sub, C), F32))
                    continue
                bref = b[u][i * sub - 1:i * sub] if i > 0 else jnp.zeros((1, dk), F32)
                qi = q[i * sub:(i + 1) * sub] * jnp.exp2(b[u][i * sub:(i + 1) * sub] - bref)
                ke = kk[u] * jnp.exp2(jnp.where(row < n_keys[i], bref - b[u], NEG))
                parts.append(_dot_nt(qi.astype(BF16), ke.astype(BF16)))
            strips[u] = parts

        intra, kv = {}, {}
        for u in units:
            a = jnp.concatenate(strips[u], axis=0)
            if factored_diag:
                a = jnp.where(causal, a, 0.0)
            else:
                b4 = b[u].reshape(nsub, sub, dk)
                q4 = blk(q_ref, u).astype(F32).reshape(nsub, sub, dk)
                k4 = kk[u].reshape(nsub, sub, dk)
                z = [(q4 * jnp.exp2(jnp.minimum(b4 - b4[:, s:s + 1], 0.0)) * k4[:, s:s + 1])
                     .reshape(C, dk).astype(BF16) for s in range(sub)]
                a = jnp.where(diag_mask, _dot(jnp.concatenate(z, axis=1), sel), a)
            v = blk(v_ref, u)
            intra[u] = _dot(a.astype(BF16), v)
            kv[u] = _dot_tn(v, (kk[u] * jnp.exp2(b[u][C - 1:C] - b[u])).astype(BF16))

        out = {}
        for hh in range(D // dk):
            st = src_ref[hh]
            for ci in range(nrows // C):
                u = (hh, ci)
                out[u] = _dot_nt(qe[u], st.astype(BF16)) + intra[u]
                st = st * jnp.exp2(b[u][C - 1:C]) + kv[u]
            stn_ref[hh] = st

        for u in units:
            hh, ci = u
            on = _rms(out[u], og_ref[...]) * blk(sg_ref, u).astype(F32)
            o_ref[0, rows_of(ci), hh * dk:(hh + 1) * dk] = on.astype(o_ref.dtype)

    worst = -jnp.min(jnp.sum(lf_ref[0].reshape(R // sub, sub, D), axis=1))
    run(True, 0, R, st_ref)

    @pl.when(worst > HGRN_SAFE_DECAY)
    def _():
        stn_ref[...] = st_ref[...]

        def piece(i, _):
            run(False, pl.multiple_of(i * HGRN_RERUN_ROWS, HGRN_RERUN_ROWS), HGRN_RERUN_ROWS, stn_ref)
            return 0

        lax.fori_loop(0, R // HGRN_RERUN_ROWS, piece, 0)

    st_ref[...] = stn_ref[...]


def _hgrn_rec(q, lf, v, sg, o_gain):
    B, Lp, D = q.shape
    dk = o_gain.shape[0]
    blk = pl.BlockSpec((1, HGRN_ROWS, D), lambda b, t: (b, t, 0))
    return pl.pallas_call(
        functools.partial(_hgrn_rec_kernel, dk=dk),
        grid=(B, Lp // HGRN_ROWS),
        in_specs=[blk, blk, blk, blk, _resident((1, dk))],
        out_specs=blk,
        out_shape=jax.ShapeDtypeStruct((B, Lp, D), BF16),
        scratch_shapes=[pltpu.VMEM((D // dk, dk, dk), F32)] * 2,
        compiler_params=_cparams("arbitrary", "arbitrary"),
        name="hgrn_rec",
    )(q, lf, v, sg, o_gain.reshape(1, dk))


def kernel(x, meta_tokens, fox_norm, fox_w_in, fox_b_f, fox_q_gain, fox_k_gain, fox_w_out,
           hgrn_norm, hgrn_w_in, hgrn_lower_bounds, hgrn_o_gain, hgrn_w_out,
           ffn_norm, ffn_w_gate, ffn_w_up, ffn_conv_w, ffn_conv_b, ffn_w_down):
    B, S, D = x.shape
    L = S + N_META
    Lp = -(-L // TOK) * TOK
    depth = ffn_norm.shape[0]
    heads = fox_b_f.shape[1]
    hd = D // heads
    assert LANE % hd == 0 and hd + N_EXTRA <= LANE and D % LANE == 0
    assert Lp % FOX_ROWS == 0 and (B * Lp) % HGRN_IN_ROWS == 0 and HGRN_IN_ROWS % TOK == 0
    assert Lp % FFN_ROWS == 0 and FFN_ROWS % FFN_SUB == 0 and Lp % HGRN_ROWS == 0 and HGRN_ROWS % HGRN_RERUN_ROWS == 0 and HGRN_RERUN_ROWS % HGRN_CHUNK == 0 and (Lp // TOK) % KEY_BLOCKS == 0 and (Lp // TOK) % Q_BLOCKS == 0 and KEY_BLOCKS % Q_BLOCKS == 0

    meta = jnp.broadcast_to(meta_tokens[None].astype(x.dtype), (B, N_META, D))
    h = jnp.concatenate([meta, x, jnp.zeros((B, Lp - L, D), x.dtype)], axis=1)

    for layer in range(depth):
        j = layer // 2
        if layer % 2 == 0:
            w_t = fox_w_in[j].astype(BF16).T
            qt, k, vt, gt = _fox_in(h, fox_norm[j], w_t, fox_b_f[j],
                                    fox_q_gain[j], fox_k_gain[j], heads)
            k = k.reshape(B, heads, -1, KEY_BLOCKS * TOK, LANE)
            og, w_out = _fox_attn(qt, k, vt, gt, fox_q_gain[j], fox_k_gain[j], hd), fox_w_out[j]
        else:
            q, lf, v, sg = _hgrn_in(h, hgrn_norm[j], hgrn_w_in[j].astype(BF16),
                                    hgrn_lower_bounds, layer)
            og, w_out = _hgrn_rec(q, lf, v, sg, hgrn_o_gain[j]), hgrn_w_out[j]
        h = _ffn(h, og, w_out, ffn_norm[layer], ffn_w_gate[layer], ffn_w_up[layer],
                 ffn_conv_w[layer], ffn_conv_b[layer], ffn_w_down[layer],
                 keep=(N_META, S) if layer == depth - 1 else None)
    return h
```

```python
import functools

import jax
import jax.numpy as jnp
from jax import lax
from jax.experimental import pallas as pl
from jax.experimental.pallas import tpu as pltpu

N_META = 16
EPS = 1e-6
NEG = -1e30
LANE = 128
SUBLANE = 8
TOK = 256
KEY_BLOCKS = 3
Q_BLOCKS = 3
FOX_ROWS = 768
FFN_ROWS = 768
FFN_SUB = 256
FFN_CHUNK = 256
HGRN_IN_ROWS = 768
HGRN_CHUNK = 64
HGRN_SUB = 16
HGRN_ROWS = 256
HGRN_RERUN_ROWS = 128
HGRN_SAFE_DECAY = 60.0
LOG2E = 1.4426950408889634
N_EXTRA = 16
ATT_TRIP = 4
ATT_SAFE_GAP = 96.0
VMEM_LIMIT = 56 * 1024 * 1024

F32 = jnp.float32
BF16 = jnp.bfloat16


def _cparams(*sem):
    return pltpu.CompilerParams(dimension_semantics=sem, vmem_limit_bytes=VMEM_LIMIT)


def _resident(shape):
    zeros = (0,) * len(shape)
    return pl.BlockSpec(shape, lambda *_: zeros, pipeline_mode=pl.Buffered(1))


def _split3(x):
    hi = x.astype(BF16)
    r = x - hi.astype(F32)
    mid = r.astype(BF16)
    lo = (r - mid.astype(F32)).astype(BF16)
    return hi, mid, lo


def _dot(a, b):
    return jnp.dot(a, b, preferred_element_type=F32)


def _dot_nt(a, b):
    return lax.dot_general(a, b, (((1,), (1,)), ((), ())), preferred_element_type=F32)


def _dot_tn(a, b):
    return lax.dot_general(a, b, (((0,), (0,)), ((), ())), preferred_element_type=F32)


def _rms(x, gain):
    ms = jnp.mean(x * x, axis=-1, keepdims=True)
    return x * lax.rsqrt(ms + EPS) * gain


def _fox_in_kernel(h_ref, g_ref, w_ref, bf_ref, qg_ref, kg_ref,
                   qt_ref, k_ref, vt_ref, gt_ref, carry_ref, *, heads):
    @pl.when(pl.program_id(1) == 0)
    def _():
        carry_ref[...] = jnp.zeros_like(carry_ref)

    R, D = h_ref.shape[1], h_ref.shape[2]
    T = TOK
    hd = D // heads
    xnts = [_rms(h_ref[0, r * T:(r + 1) * T], g_ref[...]).T.astype(BF16) for r in range(R // T)]

    src = lax.broadcasted_iota(jnp.int32, (T, T), 0)
    dst = lax.broadcasted_iota(jnp.int32, (T, T), 1)
    upper = jnp.where(src <= dst, 1.0, 0.0).astype(BF16)
    row = lax.broadcasted_iota(jnp.int32, (N_EXTRA, T), 0)
    zpad = jnp.zeros((LANE - hd - N_EXTRA, T), F32)
    ones_row = jnp.where(row == 0, 1.0, 0.0)
    carry = carry_ref[:, 0:1]

    def head_norm(y, gain_ref, scale):
        y = y.reshape(heads, hd, T)
        ms = jnp.mean(y * y, axis=1, keepdims=True)
        return y * lax.rsqrt(ms + EPS) * (gain_ref[...] * scale)

    def project(xnt):
        f0 = 3 * D
        return ([_dot(w_ref[i * D:(i + 1) * D, :], xnt) for i in range(3)]
                + [_dot(w_ref[f0:f0 + heads, :], xnt), _dot(w_ref[f0 + heads:f0 + heads + D, :], xnt)])

    nxt = project(xnts[0])
    for r in range(len(xnts)):
        cols = slice(r * T, (r + 1) * T)
        yq, yk, yv, fl, yg = nxt
        if r + 1 < len(xnts):
            nxt = project(xnts[r + 1])

        fl = fl + bf_ref[...]
        lf = jnp.minimum(fl, 0.0) - jnp.log(1.0 + jnp.exp(-jnp.abs(fl)))
        hi, mid, lo = _split3(lf)
        c = carry + (_dot(hi, upper) + _dot(mid, upper) + _dot(lo, upper))
        carry = carry + jnp.sum(lf, axis=1, keepdims=True)
        chi, cmid, clo = (p.astype(F32) for p in _split3(c * LOG2E))

        qn = head_norm(yq, qg_ref, hd ** -0.5 * LOG2E)
        kn = head_norm(yk, kg_ref, 1.0)
        vv = yv.reshape(heads, hd, T)
        gt_ref[0, :, cols] = jax.nn.sigmoid(yg).astype(gt_ref.dtype)

        for hh in range(heads):
            a, b, d = (jnp.broadcast_to(p[hh:hh + 1], (N_EXTRA, T)) for p in (chi, cmid, clo))
            eq = jnp.where(row == 0, a, jnp.where(row == 1, b, jnp.where(row == 2, d, jnp.where(row < 6, 1.0, 0.0))))
            ek = jnp.where(row < 3, 1.0, jnp.where(row == 3, -a, jnp.where(row == 4, -b, jnp.where(row == 5, -d, 0.0))))
            qt_ref[0, hh, 0, :, cols] = jnp.concatenate([qn[hh], eq, zpad], axis=0).astype(qt_ref.dtype)
            k_ref[0, hh, r] = jnp.concatenate([kn[hh], ek, zpad], axis=0).T.astype(k_ref.dtype)
            vt_ref[0, hh, 0, :, cols] = jnp.concatenate([vv[hh], ones_row], axis=0).astype(vt_ref.dtype)

    carry_ref[...] = jnp.broadcast_to(carry, carry_ref.shape)


def _fox_in(h, gain, w_t, b_f, q_gain, k_gain, heads):
    B, Lp, D = h.shape
    hd = D // heads
    R = FOX_ROWS
    nR, sub = Lp // R, R // TOK
    assert R == Q_BLOCKS * TOK == KEY_BLOCKS * TOK
    return pl.pallas_call(
        functools.partial(_fox_in_kernel, heads=heads),
        grid=(B, nR),
        in_specs=[
            pl.BlockSpec((1, R, D), lambda b, t: (b, t, 0)),
            _resident((1, D)), _resident(w_t.shape),
            _resident((heads, 1)), _resident((hd, 1)), _resident((hd, 1)),
        ],
        out_specs=[
            pl.BlockSpec((1, heads, 1, LANE, R), lambda b, t: (b, 0, t, 0, 0)),
            pl.BlockSpec((1, heads, sub, TOK, LANE), lambda b, t: (b, 0, t, 0, 0)),
            pl.BlockSpec((1, heads, 1, hd + N_EXTRA, R), lambda b, t: (b, 0, t, 0, 0)),
            pl.BlockSpec((1, D, R), lambda b, t: (b, 0, t)),
        ],
        out_shape=[
            jax.ShapeDtypeStruct((B, heads, nR, LANE, R), BF16),
            jax.ShapeDtypeStruct((B, heads, nR * sub, TOK, LANE), BF16),
            jax.ShapeDtypeStruct((B, heads, nR, hd + N_EXTRA, R), BF16),
            jax.ShapeDtypeStruct((B, D, Lp), BF16),
        ],
        scratch_shapes=[pltpu.VMEM((heads, LANE), F32)],
        compiler_params=_cparams("arbitrary", "arbitrary"),
        name="fox_in",
    )(h, gain.reshape(1, D), w_t, b_f.reshape(heads, 1),
      q_gain.reshape(hd, 1), k_gain.reshape(hd, 1))


def _fox_attn_kernel(bound_ref, qt_ref, k_ref, vt_ref, gt_ref, o_ref, s_ref, cm_ref, m_ref, acc_ref, *, hd):
    qi = pl.program_id(2)
    T = qt_ref.shape[-1]
    pair = qt_ref.shape[1]
    CK = k_ref.shape[3]
    n_full = (qi * T) // CK
    assert CK == T
    bound = bound_ref[0]

    def logits(c, j, masked):
        s = _dot(k_ref[0, j, c], qt_ref[0, j, 0])
        if masked:
            key = c * CK - qi * T + lax.broadcasted_iota(jnp.int32, (CK, T), 0)
            qry = lax.broadcasted_iota(jnp.int32, (CK, T), 1)
            s = jnp.where(key <= qry, s, NEG)
        return s

    def finalize():
        o = jnp.concatenate([acc_ref[j, :hd] / acc_ref[j, hd:hd + 1] for j in range(pair)], axis=0)
        o_ref[0] = (o * gt_ref[0].astype(F32)).T.astype(o_ref.dtype)

    def bounded():
        acc_ref[...] = jnp.zeros(acc_ref.shape, F32)

        def diag_logits(c, j):
            parts = []
            for lo in range(0, CK, TOK):
                s = _dot(k_ref[0, j, c, lo:lo + TOK, :], qt_ref[0, j, 0, :, lo:])
                key = lax.broadcasted_iota(jnp.int32, s.shape, 0)
                qry = lax.broadcasted_iota(jnp.int32, s.shape, 1)
                parts.append(jnp.where(key <= qry, s, NEG))
            return parts

        def sweep(chunks):
            units = [(c, j, diag) for c, diag in chunks for j in range(pair)]
            ahead = 2

            def unit_logits(c, j, diag):
                return diag_logits(c, j) if diag else logits(c, j, False)

            s = [unit_logits(*u) for u in units[:ahead]]
            for n, (c, j, diag) in enumerate(units):
                if n + ahead < len(units):
                    s.append(unit_logits(*units[n + ahead]))
                if diag:
                    for part, lo in zip(s[n], range(0, CK, TOK)):
                        p = jnp.exp2(part - bound).astype(BF16)
                        acc_ref[j, :, lo:] = acc_ref[j, :, lo:] + _dot(vt_ref[0, j, c, :, lo:lo + TOK], p)
                else:
                    p = jnp.exp2(s[n] - bound).astype(BF16)
                    acc_ref[j] = acc_ref[j] + _dot(vt_ref[0, j, c], p)

        rem = n_full % ATT_TRIP

        def body(i, _):
            sweep([(ATT_TRIP * i + d, False) for d in range(ATT_TRIP)])
            return 0

        lax.fori_loop(0, n_full // ATT_TRIP, body, 0)

        for r in range(ATT_TRIP):
            @pl.when(rem == r)
            def _(r=r):
                sweep([(n_full - r + d, False) for d in range(r)] + [(n_full, True)])

        finalize()

    def scores(c, slot, masked=False):
        for j in range(pair):
            s = logits(c, j, masked)
            s_ref[slot, j] = s
            cm_ref[slot, j] = jnp.max(s, axis=0, keepdims=True)

    def accumulate(c, slot):
        for j in range(pair):
            m = m_ref[j]
            m_new = jnp.maximum(m, cm_ref[slot, j])
            p = jnp.exp2(s_ref[slot, j] - m_new).astype(BF16)
            acc_ref[j] = acc_ref[j] * jnp.exp2(m - m_new) + _dot(vt_ref[0, j, c], p)
            m_ref[j] = m_new

    def online():
        m_ref[...] = jnp.full(m_ref.shape, NEG, F32)
        acc_ref[...] = jnp.zeros(acc_ref.shape, F32)

        steps = n_full - 1
        odd = steps % 2

        @pl.when(n_full == 0)
        def _():
            scores(0, 1, masked=True)

        @pl.when(jnp.logical_and(n_full > 0, odd == 0))
        def _():
            scores(0, 0)

        @pl.when(jnp.logical_and(n_full > 0, odd == 1))
        def _():
            scores(0, 1)
            scores(1, 0)
            accumulate(0, 1)

        def body(i, _):
            c = odd + 2 * i
            scores(c + 1, 1)
            accumulate(c, 0)
            scores(c + 2, 0)
            accumulate(c + 1, 1)
            return 0

        lax.fori_loop(0, steps // 2, body, 0)

        @pl.when(n_full > 0)
        def _():
            scores(n_full, 1, masked=True)
            accumulate(n_full - 1, 0)

        accumulate(n_full, 1)
        finalize()

    bounded()

    @pl.when(2.0 * bound > ATT_SAFE_GAP)
    def _():
        online()


def _fox_attn(qt, k, vt, gt, q_gain, k_gain, hd):
    B, heads, nQ = qt.shape[:3]
    TQ = qt.shape[-1]
    nC, CK = k.shape[2], k.shape[3]
    D, Lp = gt.shape[1], gt.shape[2]
    pair = LANE // hd
    bound = jnp.max(jnp.abs(q_gain)) * jnp.max(jnp.abs(k_gain)) * (hd ** 0.5 * LOG2E)
    return pl.pallas_call(
        functools.partial(_fox_attn_kernel, hd=hd),
        grid=(B, heads // pair, nQ),
        in_specs=[
            pl.BlockSpec(memory_space=pltpu.SMEM),
            pl.BlockSpec((1, pair, 1, LANE, TQ), lambda b, p, q: (b, p, q, 0, 0)),
            pl.BlockSpec((1, pair, nC, CK, LANE), lambda b, p, q: (b, p, 0, 0, 0)),
            pl.BlockSpec((1, pair, nC, vt.shape[3], CK), lambda b, p, q: (b, p, 0, 0, 0)),
            pl.BlockSpec((1, LANE, TQ), lambda b, p, q: (b, p, q)),
        ],
        out_specs=pl.BlockSpec((1, TQ, LANE), lambda b, p, q: (b, q, p)),
        out_shape=jax.ShapeDtypeStruct((B, Lp, D), BF16),
        scratch_shapes=[
            pltpu.VMEM((2, pair, CK, TQ), F32),
            pltpu.VMEM((2, pair, 1, TQ), F32),
            pltpu.VMEM((pair, 1, TQ), F32),
            pltpu.VMEM((pair, vt.shape[3], TQ), F32),
        ],
        compiler_params=_cparams("arbitrary", "arbitrary", "arbitrary"),
        name="fox_attn",
    )(bound.reshape(1).astype(F32), qt, k, vt, gt)


def _ffn_kernel(h_ref, x_ref, wo_ref, g_ref, wg_ref, wu_ref, cw_ref, wd_ref, o_ref, carry_ref, *stage,
                skip_rows=0, keep_rows=0, n_b=0, n_t=0):
    @pl.when(pl.program_id(1) == 0)
    def _():
        carry_ref[...] = jnp.zeros_like(carry_ref)

    T = h_ref.shape[1]
    F = wg_ref.shape[1]
    S = FFN_SUB
    n_chunks = F // FFN_CHUNK
    row = lax.broadcasted_iota(jnp.int32, (S, FFN_CHUNK), 0)

    if stage:
        buf_ref, sem_ref = stage
        n_steps = n_b * n_t
        step = pl.program_id(0) * n_t + pl.program_id(1)
        slot = step % 2
        last_rows = skip_rows + keep_rows - (n_t - 1) * T

        def copy_out(st, start):
            b, t = st // n_t, st % n_t
            sl = st % 2

            def go(src, dst):
                cp = pltpu.make_async_copy(buf_ref.at[sl, src], o_ref.at[b, dst], sem_ref.at[sl])
                cp.start() if start else cp.wait()

            @pl.when(t == 0)
            def _():
                go(pl.ds(skip_rows, T - skip_rows), pl.ds(0, T - skip_rows))

            @pl.when(jnp.logical_and(t > 0, t < n_t - 1))
            def _():
                go(pl.ds(0, T), pl.ds(pl.multiple_of(t * T - skip_rows, SUBLANE), T))

            @pl.when(jnp.logical_and(t == n_t - 1, t > 0))
            def _():
                go(pl.ds(0, last_rows), pl.ds(keep_rows - last_rows, last_rows))

        @pl.when(step >= 2)
        def _():
            copy_out(step - 2, start=False)

    h1s = [h_ref[0, r * S:(r + 1) * S] + _dot(x_ref[0, r * S:(r + 1) * S], wo_ref[...]) for r in range(T // S)]
    xns = [_rms(h1, g_ref[...]).astype(BF16) for h1 in h1s]
    tails = [carry_ref[:, c * FFN_CHUNK:(c + 1) * FFN_CHUNK] for c in range(n_chunks)]

    for r, xn in enumerate(xns):
        def gate_up(c):
            sl = slice(c * FFN_CHUNK, (c + 1) * FFN_CHUNK)
            return _dot(xn, wg_ref[:, sl]), _dot(xn, wu_ref[:, sl])

        acc = h1s[r]
        nxt = gate_up(0)
        for c in range(n_chunks):
            sl = slice(c * FFN_CHUNK, (c + 1) * FFN_CHUNK)
            a, u = nxt
            if c + 1 < n_chunks:
                nxt = gate_up(c + 1)
            prev = tails[c]
            a1 = jnp.where(row == 0, prev[SUBLANE - 1:SUBLANE], pltpu.roll(a, 1, 0))
            a2 = jnp.where(row == 0, prev[SUBLANE - 2:SUBLANE - 1],
                           jnp.where(row == 1, prev[SUBLANE - 1:SUBLANE], pltpu.roll(a, 2, 0)))
            tails[c] = a[S - SUBLANE:S]
            cw = cw_ref[:, sl]
            z = cw[3:4] + cw[0:1] * a2 + cw[1:2] * a1 + cw[2:3] * a
            act = (z * jax.nn.sigmoid(z) * u).astype(BF16)
            acc = acc + _dot(act, wd_ref[sl, :])
        if stage:
            stage[0][slot, r * S:(r + 1) * S] = acc
        else:
            o_ref[0, r * S:(r + 1) * S] = acc

    for c in range(n_chunks):
        carry_ref[:, c * FFN_CHUNK:(c + 1) * FFN_CHUNK] = tails[c]

    if stage:
        copy_out(step, start=True)

        @pl.when(step == n_steps - 1)
        def _():
            if n_steps > 1:
                copy_out(step - 1, start=False)
            copy_out(step, start=False)


def _ffn(h, x, w_out, gain, w_gate, w_up, conv_w, conv_b, w_down, keep=None):
    B, Lp, D = h.shape
    F = w_gate.shape[1]
    cw = jnp.concatenate([conv_w, conv_b[None], jnp.zeros((SUBLANE - conv_w.shape[0] - 1, F), F32)], axis=0)
    if keep is None:
        kern = _ffn_kernel
        out_spec = pl.BlockSpec((1, FFN_ROWS, D), lambda b, t: (b, t, 0))
        out_shape = jax.ShapeDtypeStruct((B, Lp, D), F32)
        staging = []
    else:
        skip_rows, keep_rows = keep
        assert skip_rows % SUBLANE == 0 and 0 < skip_rows < FFN_ROWS and Lp // FFN_ROWS >= 2
        assert (Lp // FFN_ROWS - 1) * FFN_ROWS < skip_rows + keep_rows <= Lp
        kern = functools.partial(_ffn_kernel, skip_rows=skip_rows, keep_rows=keep_rows, n_b=B, n_t=Lp // FFN_ROWS)
        out_spec = pl.BlockSpec(memory_space=pl.ANY)
        out_shape = jax.ShapeDtypeStruct((B, keep_rows, D), F32)
        staging = [pltpu.VMEM((2, FFN_ROWS, D), F32), pltpu.SemaphoreType.DMA((2,))]
    return pl.pallas_call(
        kern,
        grid=(B, Lp // FFN_ROWS),
        in_specs=[
            pl.BlockSpec((1, FFN_ROWS, D), lambda b, t: (b, t, 0)),
            pl.BlockSpec((1, FFN_ROWS, D), lambda b, t: (b, t, 0)),
            _resident((D, D)), _resident((1, D)), _resident((D, F)), _resident((D, F)),
            _resident((SUBLANE, F)), _resident((F, D)),
        ],
        out_specs=out_spec,
        out_shape=out_shape,
        scratch_shapes=[pltpu.VMEM((SUBLANE, F), F32)] + staging,
        compiler_params=_cparams("arbitrary", "arbitrary"),
        name="ffn",
    )(h, x, w_out.astype(BF16), gain.reshape(1, D), w_gate.astype(BF16), w_up.astype(BF16), cw,
      w_down.astype(BF16))


def _hgrn_in_kernel(h_ref, g_ref, w_ref, lbs_ref, q_ref, lf_ref, v_ref, sg_ref, *, layer):
    R, D = h_ref.shape
    T = TOK
    lbs = lbs_ref[...]
    e = jnp.exp(lbs - jnp.max(lbs, axis=0, keepdims=True))
    p = e / jnp.sum(e, axis=0, keepdims=True)
    lb = jnp.sum(p[1:layer + 1], axis=0, keepdims=True) if layer > 0 else jnp.zeros((1, D), F32)

    xns = [_rms(h_ref[r * T:(r + 1) * T], g_ref[...]).astype(BF16) for r in range(R // T)]
    def project(xn):
        return [_dot(xn, w_ref[:, i * D:(i + 1) * D]) for i in range(4)]

    nxt = project(xns[0])
    for r in range(len(xns)):
        rows = slice(r * T, (r + 1) * T)
        yq, yf, yi, yg = nxt
        if r + 1 < len(xns):
            nxt = project(xns[r + 1])
        q_ref[rows] = (yq * jax.nn.sigmoid(yq)).astype(q_ref.dtype)
        f = lb + (1.0 - lb) * jax.nn.sigmoid(yf)
        lf_ref[rows] = jnp.log(f)
        v_ref[rows] = yi.astype(v_ref.dtype)
        sg_ref[rows] = jax.nn.sigmoid(yg).astype(sg_ref.dtype)


def _hgrn_in(h, gain, w, lower_bounds, layer):
    B, Lp, D = h.shape
    rows = B * Lp
    blk = pl.BlockSpec((HGRN_IN_ROWS, D), lambda i: (i, 0))
    outs = pl.pallas_call(
        functools.partial(_hgrn_in_kernel, layer=layer),
        grid=(rows // HGRN_IN_ROWS,),
        in_specs=[blk, _resident((1, D)), _resident(w.shape), _resident(lower_bounds.shape)],
        out_specs=[blk, blk, blk, blk],
        out_shape=[
            jax.ShapeDtypeStruct((rows, D), BF16),
            jax.ShapeDtypeStruct((rows, D), F32),
            jax.ShapeDtypeStruct((rows, D), BF16),
            jax.ShapeDtypeStruct((rows, D), BF16),
        ],
        compiler_params=_cparams("arbitrary"),
        name="hgrn_in",
    )(h.reshape(rows, D), gain.reshape(1, D), w, lower_bounds)
    return [o.reshape(B, Lp, D) for o in outs]


def _hgrn_rec_kernel(q_ref, lf_ref, v_ref, sg_ref, og_ref, o_ref, st_ref, stn_ref, *, dk):
    @pl.when(pl.program_id(1) == 0)
    def _():
        st_ref[...] = jnp.zeros_like(st_ref)

    C = HGRN_CHUNK
    R, D = q_ref.shape[1], q_ref.shape[2]
    sub = HGRN_SUB
    nsub = C // sub
    r_cc = lax.broadcasted_iota(jnp.int32, (C, C), 0)
    c_cc = lax.broadcasted_iota(jnp.int32, (C, C), 1)
    causal = c_cc <= r_cc
    ltri = jnp.where(causal, 1.0, 0.0).astype(BF16)
    row = lax.broadcasted_iota(jnp.int32, (C, dk), 0)

    def run(chunk_factored, row0, nrows, src_ref):
        if not chunk_factored:
            diag_mask = jnp.logical_and(c_cc >= r_cc - (r_cc & (sub - 1)), causal)
            d_sel = (lax.broadcasted_iota(jnp.int32, (sub * dk, C), 0)
                     - (lax.broadcasted_iota(jnp.int32, (sub * dk, C), 1) & (sub - 1)) * dk)
            sel = jnp.where(jnp.logical_and(d_sel >= 0, d_sel < dk), 1.0, 0.0).astype(BF16)

        units = [(hh, ci) for hh in range(D // dk) for ci in range(nrows // C)]

        def rows_of(ci):
            return pl.ds(row0 + ci * C, C)

        def blk(ref, u):
            hh, ci = u
            return ref[0, rows_of(ci), hh * dk:(hh + 1) * dk]

        b = {}
        for u in units:
            hi, mid, lo = _split3(blk(lf_ref, u))
            b[u] = (_dot(ltri, hi) + _dot(ltri, mid) + _dot(ltri, lo)) * LOG2E

        qe, kk, ke, scores = {}, {}, {}, {}
        for u in units:
            q = blk(q_ref, u).astype(F32)
            kk[u] = 1.0 - jnp.exp(blk(lf_ref, u))
            qe[u] = (q * jnp.exp2(b[u])).astype(BF16)
            if chunk_factored:
                ke[u] = kk[u] * jnp.exp2(-b[u])
                scores[u] = _dot_nt(qe[u], ke[u].astype(BF16))
            else:
                parts = [jnp.zeros((sub, C), F32)]
                for i in range(1, nsub):
                    bref = b[u][i * sub - 1:i * sub]
                    qi = q[i * sub:(i + 1) * sub] * jnp.exp2(b[u][i * sub:(i + 1) * sub] - bref)
                    ks = kk[u] * jnp.exp2(jnp.where(row < i * sub, bref - b[u], NEG))
                    parts.append(_dot_nt(qi.astype(BF16), ks.astype(BF16)))
                scores[u] = jnp.concatenate(parts, axis=0)

        intra, kv = {}, {}
        for u in units:
            if chunk_factored:
                a = jnp.where(causal, scores[u], 0.0)
                ke_end = ke[u] * jnp.exp2(b[u][C - 1:C])
            else:
                b4 = b[u].reshape(nsub, sub, dk)
                q4 = blk(q_ref, u).astype(F32).reshape(nsub, sub, dk)
                k4 = kk[u].reshape(nsub, sub, dk)
                z = [(q4 * jnp.exp2(jnp.minimum(b4 - b4[:, s:s + 1], 0.0)) * k4[:, s:s + 1])
                     .reshape(C, dk).astype(BF16) for s in range(sub)]
                a = jnp.where(diag_mask, _dot(jnp.concatenate(z, axis=1), sel), scores[u])
                ke_end = kk[u] * jnp.exp2(b[u][C - 1:C] - b[u])
            v = blk(v_ref, u)
            intra[u] = _dot(a.astype(BF16), v)
            kv[u] = _dot_tn(v, ke_end.astype(BF16))

        out = {}
        for hh in range(D // dk):
            st = src_ref[hh]
            for ci in range(nrows // C):
                u = (hh, ci)
                out[u] = _dot_nt(qe[u], st.astype(BF16)) + intra[u]
                st = st * jnp.exp2(b[u][C - 1:C]) + kv[u]
            stn_ref[hh] = st

        for u in units:
            hh, ci = u
            on = _rms(out[u], og_ref[...]) * blk(sg_ref, u).astype(F32)
            o_ref[0, rows_of(ci), hh * dk:(hh + 1) * dk] = on.astype(o_ref.dtype)

    worst = -jnp.min(jnp.sum(lf_ref[0].reshape(R // C, C, D), axis=1))
    run(True, 0, R, st_ref)

    @pl.when(worst > HGRN_SAFE_DECAY)
    def _():
        stn_ref[...] = st_ref[...]

        def piece(i, _):
            run(False, pl.multiple_of(i * HGRN_RERUN_ROWS, HGRN_RERUN_ROWS), HGRN_RERUN_ROWS, stn_ref)
            return 0

        lax.fori_loop(0, R // HGRN_RERUN_ROWS, piece, 0)

    st_ref[...] = stn_ref[...]


def _hgrn_rec(q, lf, v, sg, o_gain):
    B, Lp, D = q.shape
    dk = o_gain.shape[0]
    blk = pl.BlockSpec((1, HGRN_ROWS, D), lambda b, t: (b, t, 0))
    return pl.pallas_call(
        functools.partial(_hgrn_rec_kernel, dk=dk),
        grid=(B, Lp // HGRN_ROWS),
        in_specs=[blk, blk, blk, blk, _resident((1, dk))],
        out_specs=blk,
        out_shape=jax.ShapeDtypeStruct((B, Lp, D), BF16),
        scratch_shapes=[pltpu.VMEM((D // dk, dk, dk), F32)] * 2,
        compiler_params=_cparams("arbitrary", "arbitrary"),
        name="hgrn_rec",
    )(q, lf, v, sg, o_gain.reshape(1, dk))


def kernel(x, meta_tokens, fox_norm, fox_w_in, fox_b_f, fox_q_gain, fox_k_gain, fox_w_out,
           hgrn_norm, hgrn_w_in, hgrn_lower_bounds, hgrn_o_gain, hgrn_w_out,
           ffn_norm, ffn_w_gate, ffn_w_up, ffn_conv_w, ffn_conv_b, ffn_w_down):
    B, S, D = x.shape
    L = S + N_META
    Lp = -(-L // TOK) * TOK
    depth = ffn_norm.shape[0]
    heads = fox_b_f.shape[1]
    hd = D // heads
    assert LANE % hd == 0 and hd + N_EXTRA <= LANE and D % LANE == 0
    assert Lp % FOX_ROWS == 0 and (B * Lp) % HGRN_IN_ROWS == 0 and HGRN_IN_ROWS % TOK == 0
    assert Lp % FFN_ROWS == 0 and FFN_ROWS % FFN_SUB == 0 and Lp % HGRN_ROWS == 0 and HGRN_ROWS % HGRN_RERUN_ROWS == 0 and HGRN_RERUN_ROWS % HGRN_CHUNK == 0 and (Lp // TOK) % KEY_BLOCKS == 0 and (Lp // TOK) % Q_BLOCKS == 0 and KEY_BLOCKS % Q_BLOCKS == 0

    meta = jnp.broadcast_to(meta_tokens[None].astype(x.dtype), (B, N_META, D))
    h = jnp.concatenate([meta, x, jnp.zeros((B, Lp - L, D), x.dtype)], axis=1)

    for layer in range(depth):
        j = layer // 2
        if layer % 2 == 0:
            w_t = fox_w_in[j].astype(BF16).T
            qt, k, vt, gt = _fox_in(h, fox_norm[j], w_t, fox_b_f[j],
                                    fox_q_gain[j], fox_k_gain[j], heads)
            k = k.reshape(B, heads, -1, KEY_BLOCKS * TOK, LANE)
            og, w_out = _fox_attn(qt, k, vt, gt, fox_q_gain[j], fox_k_gain[j], hd), fox_w_out[j]
        else:
            q, lf, v, sg = _hgrn_in(h, hgrn_norm[j], hgrn_w_in[j].astype(BF16),
                                    hgrn_lower_bounds, layer)
            og, w_out = _hgrn_rec(q, lf, v, sg, hgrn_o_gain[j]), hgrn_w_out[j]
        h = _ffn(h, og, w_out, ffn_norm[layer], ffn_w_gate[layer], ffn_w_up[layer],
                 ffn_conv_w[layer], ffn_conv_b[layer], ffn_w_down[layer],
                 keep=(N_META, S) if layer == depth - 1 else None)
    return h
```

```python
import functools

import jax
import jax.numpy as jnp
from jax import lax
from jax.experimental import pallas as pl
from jax.experimental.pallas import tpu as pltpu

N_META = 16
EPS = 1e-6
NEG = -1e30
LANE = 128
SUBLANE = 8
TOK = 256
KEY_BLOCKS = 3
Q_BLOCKS = 3
FOX_ROWS = 768
FFN_ROWS = 768
FFN_SUB = 256
FFN_CHUNK = 256
HGRN_IN_ROWS = 768
HGRN_CHUNK = 64
HGRN_SUB = 16
HGRN_ROWS = 256
HGRN_RERUN_ROWS = 128
HGRN_SAFE_DECAY = 60.0
LOG2E = 1.4426950408889634
N_EXTRA = 16
ATT_TRIP = 4
ATT_SAFE_GAP = 96.0
VMEM_LIMIT = 56 * 1024 * 1024

F32 = jnp.float32
BF16 = jnp.bfloat16


def _cparams(*sem):
    return pltpu.CompilerParams(dimension_semantics=sem, vmem_limit_bytes=VMEM_LIMIT)


def _resident(shape):
    zeros = (0,) * len(shape)
    return pl.BlockSpec(shape, lambda *_: zeros, pipeline_mode=pl.Buffered(1))


def _split3(x):
    hi = x.astype(BF16)
    r = x - hi.astype(F32)
    mid = r.astype(BF16)
    lo = (r - mid.astype(F32)).astype(BF16)
    return hi, mid, lo


def _dot(a, b):
    return jnp.dot(a, b, preferred_element_type=F32)


def _dot_nt(a, b):
    return lax.dot_general(a, b, (((1,), (1,)), ((), ())), preferred_element_type=F32)


def _dot_tn(a, b):
    return lax.dot_general(a, b, (((0,), (0,)), ((), ())), preferred_element_type=F32)


def _rms(x, gain):
    ms = jnp.mean(x * x, axis=-1, keepdims=True)
    return x * lax.rsqrt(ms + EPS) * gain


def _fox_in_kernel(h_ref, g_ref, w_ref, bf_ref, qg_ref, kg_ref,
                   qt_ref, k_ref, vt_ref, gt_ref, carry_ref, *, heads):
    @pl.when(pl.program_id(1) == 0)
    def _():
        carry_ref[...] = jnp.zeros_like(carry_ref)

    R, D = h_ref.shape[1], h_ref.shape[2]
    T = TOK
    hd = D // heads
    xnts = [_rms(h_ref[0, r * T:(r + 1) * T], g_ref[...]).T.astype(BF16) for r in range(R // T)]

    src = lax.broadcasted_iota(jnp.int32, (T, T), 0)
    dst = lax.broadcasted_iota(jnp.int32, (T, T), 1)
    upper = jnp.where(src <= dst, 1.0, 0.0).astype(BF16)
    row = lax.broadcasted_iota(jnp.int32, (N_EXTRA, T), 0)
    zpad = jnp.zeros((LANE - hd - N_EXTRA, T), F32)
    ones_row = jnp.where(row == 0, 1.0, 0.0)
    carry = carry_ref[:, 0:1]

    def head_norm(y, gain_ref, scale):
        y = y.reshape(heads, hd, T)
        ms = jnp.mean(y * y, axis=1, keepdims=True)
        return y * lax.rsqrt(ms + EPS) * (gain_ref[...] * scale)

    def project(xnt):
        f0 = 3 * D
        return ([_dot(w_ref[i * D:(i + 1) * D, :], xnt) for i in range(3)]
                + [_dot(w_ref[f0:f0 + heads, :], xnt), _dot(w_ref[f0 + heads:f0 + heads + D, :], xnt)])

    nxt = project(xnts[0])
    for r in range(len(xnts)):
        cols = slice(r * T, (r + 1) * T)
        yq, yk, yv, fl, yg = nxt
        if r + 1 < len(xnts):
            nxt = project(xnts[r + 1])

        fl = fl + bf_ref[...]
        lf = jnp.minimum(fl, 0.0) - jnp.log(1.0 + jnp.exp(-jnp.abs(fl)))
        hi, mid, lo = _split3(lf)
        c = carry + (_dot(hi, upper) + _dot(mid, upper) + _dot(lo, upper))
        carry = carry + jnp.sum(lf, axis=1, keepdims=True)
        chi, cmid, clo = (p.astype(F32) for p in _split3(c * LOG2E))

        qn = head_norm(yq, qg_ref, hd ** -0.5 * LOG2E)
        kn = head_norm(yk, kg_ref, 1.0)
        vv = yv.reshape(heads, hd, T)
        gt_ref[0, :, cols] = jax.nn.sigmoid(yg).astype(gt_ref.dtype)

        for hh in range(heads):
            a, b, d = (jnp.broadcast_to(p[hh:hh + 1], (N_EXTRA, T)) for p in (chi, cmid, clo))
            eq = jnp.where(row == 0, a, jnp.where(row == 1, b, jnp.where(row == 2, d, jnp.where(row < 6, 1.0, 0.0))))
            ek = jnp.where(row < 3, 1.0, jnp.where(row == 3, -a, jnp.where(row == 4, -b, jnp.where(row == 5, -d, 0.0))))
            qt_ref[0, hh, 0, :, cols] = jnp.concatenate([qn[hh], eq, zpad], axis=0).astype(qt_ref.dtype)
            k_ref[0, hh, r] = jnp.concatenate([kn[hh], ek, zpad], axis=0).T.astype(k_ref.dtype)
            vt_ref[0, hh, 0, :, cols] = jnp.concatenate([vv[hh], ones_row], axis=0).astype(vt_ref.dtype)

    carry_ref[...] = jnp.broadcast_to(carry, carry_ref.shape)


def _fox_in(h, gain, w_t, b_f, q_gain, k_gain, heads):
    B, Lp, D = h.shape
    hd = D // heads
    R = FOX_ROWS
    nR, sub = Lp // R, R // TOK
    assert R == Q_BLOCKS * TOK == KEY_BLOCKS * TOK
    return pl.pallas_call(
        functools.partial(_fox_in_kernel, heads=heads),
        grid=(B, nR),
        in_specs=[
            pl.BlockSpec((1, R, D), lambda b, t: (b, t, 0)),
            _resident((1, D)), _resident(w_t.shape),
            _resident((heads, 1)), _resident((hd, 1)), _resident((hd, 1)),
        ],
        out_specs=[
            pl.BlockSpec((1, heads, 1, LANE, R), lambda b, t: (b, 0, t, 0, 0)),
            pl.BlockSpec((1, heads, sub, TOK, LANE), lambda b, t: (b, 0, t, 0, 0)),
            pl.BlockSpec((1, heads, 1, hd + N_EXTRA, R), lambda b, t: (b, 0, t, 0, 0)),
            pl.BlockSpec((1, D, R), lambda b, t: (b, 0, t)),
        ],
        out_shape=[
            jax.ShapeDtypeStruct((B, heads, nR, LANE, R), BF16),
            jax.ShapeDtypeStruct((B, heads, nR * sub, TOK, LANE), BF16),
            jax.ShapeDtypeStruct((B, heads, nR, hd + N_EXTRA, R), BF16),
            jax.ShapeDtypeStruct((B, D, Lp), BF16),
        ],
        scratch_shapes=[pltpu.VMEM((heads, LANE), F32)],
        compiler_params=_cparams("arbitrary", "arbitrary"),
        name="fox_in",
    )(h, gain.reshape(1, D), w_t, b_f.reshape(heads, 1),
      q_gain.reshape(hd, 1), k_gain.reshape(hd, 1))


def _fox_attn_kernel(bound_ref, qt_ref, k_ref, vt_ref, gt_ref, o_ref, s_ref, cm_ref, m_ref, acc_ref, *, hd):
    qi = pl.program_id(2)
    T = qt_ref.shape[-1]
    pair = qt_ref.shape[1]
    CK = k_ref.shape[3]
    n_full = (qi * T) // CK
    assert CK == T
    bound = bound_ref[0]

    def logits(c, j, masked):
        s = _dot(k_ref[0, j, c], qt_ref[0, j, 0])
        if masked:
            key = c * CK - qi * T + lax.broadcasted_iota(jnp.int32, (CK, T), 0)
            qry = lax.broadcasted_iota(jnp.int32, (CK, T), 1)
            s = jnp.where(key <= qry, s, NEG)
        return s

    def finalize():
        o = jnp.concatenate([acc_ref[j, :hd] / acc_ref[j, hd:hd + 1] for j in range(pair)], axis=0)
        o_ref[0] = (o * gt_ref[0].astype(F32)).T.astype(o_ref.dtype)

    def bounded():
        acc_ref[...] = jnp.zeros(acc_ref.shape, F32)

        def diag_logits(c, j):
            parts = []
            for lo in range(0, CK, TOK):
                s = _dot(k_ref[0, j, c, lo:lo + TOK, :], qt_ref[0, j, 0, :, lo:])
                key = lax.broadcasted_iota(jnp.int32, s.shape, 0)
                qry = lax.broadcasted_iota(jnp.int32, s.shape, 1)
                parts.append(jnp.where(key <= qry, s, NEG))
            return parts

        def sweep(chunks):
            units = [(c, j, diag) for c, diag in chunks for j in range(pair)]
            ahead = 2

            def unit_logits(c, j, diag):
                return diag_logits(c, j) if diag else logits(c, j, False)

            s = [unit_logits(*u) for u in units[:ahead]]
            for n, (c, j, diag) in enumerate(units):
                if n + ahead < len(units):
                    s.append(unit_logits(*units[n + ahead]))
                if diag:
                    for part, lo in zip(s[n], range(0, CK, TOK)):
                        p = jnp.exp2(part - bound).astype(BF16)
                        acc_ref[j, :, lo:] = acc_ref[j, :, lo:] + _dot(vt_ref[0, j, c, :, lo:lo + TOK], p)
                else:
                    p = jnp.exp2(s[n] - bound).astype(BF16)
                    acc_ref[j] = acc_ref[j] + _dot(vt_ref[0, j, c], p)

        rem = n_full % ATT_TRIP

        def body(i, _):
            sweep([(ATT_TRIP * i + d, False) for d in range(ATT_TRIP)])
            return 0

        lax.fori_loop(0, n_full // ATT_TRIP, body, 0)

        for r in range(ATT_TRIP):
            @pl.when(rem == r)
            def _(r=r):
                sweep([(n_full - r + d, False) for d in range(r)] + [(n_full, True)])
                finalize()

    def scores(c, slot, masked=False):
        for j in range(pair):
            s = logits(c, j, masked)
            s_ref[slot, j] = s
            cm_ref[slot, j] = jnp.max(s, axis=0, keepdims=True)

    def accumulate(c, slot):
        for j in range(pair):
            m = m_ref[j]
            m_new = jnp.maximum(m, cm_ref[slot, j])
            p = jnp.exp2(s_ref[slot, j] - m_new).astype(BF16)
            acc_ref[j] = acc_ref[j] * jnp.exp2(m - m_new) + _dot(vt_ref[0, j, c], p)
            m_ref[j] = m_new

    def online():
        m_ref[...] = jnp.full(m_ref.shape, NEG, F32)
        acc_ref[...] = jnp.zeros(acc_ref.shape, F32)

        steps = n_full - 1
        odd = steps % 2

        @pl.when(n_full == 0)
        def _():
            scores(0, 1, masked=True)

        @pl.when(jnp.logical_and(n_full > 0, odd == 0))
        def _():
            scores(0, 0)

        @pl.when(jnp.logical_and(n_full > 0, odd == 1))
        def _():
            scores(0, 1)
            scores(1, 0)
            accumulate(0, 1)

        def body(i, _):
            c = odd + 2 * i
            scores(c + 1, 1)
            accumulate(c, 0)
            scores(c + 2, 0)
            accumulate(c + 1, 1)
            return 0

        lax.fori_loop(0, steps // 2, body, 0)

        @pl.when(n_full > 0)
        def _():
            scores(n_full, 1, masked=True)
            accumulate(n_full - 1, 0)

        accumulate(n_full, 1)
        finalize()

    bounded()

    @pl.when(2.0 * bound > ATT_SAFE_GAP)
    def _():
        online()


def _fox_attn(qt, k, vt, gt, q_gain, k_gain, hd):
    B, heads, nQ = qt.shape[:3]
    TQ = qt.shape[-1]
    nC, CK = k.shape[2], k.shape[3]
    D, Lp = gt.shape[1], gt.shape[2]
    pair = LANE // hd
    bound = jnp.max(jnp.abs(q_gain)) * jnp.max(jnp.abs(k_gain)) * (hd ** 0.5 * LOG2E)
    return pl.pallas_call(
        functools.partial(_fox_attn_kernel, hd=hd),
        grid=(B, heads // pair, nQ),
        in_specs=[
            pl.BlockSpec(memory_space=pltpu.SMEM),
            pl.BlockSpec((1, pair, 1, LANE, TQ), lambda b, p, q: (b, p, q, 0, 0)),
            pl.BlockSpec((1, pair, nC, CK, LANE), lambda b, p, q: (b, p, 0, 0, 0)),
            pl.BlockSpec((1, pair, nC, vt.shape[3], CK), lambda b, p, q: (b, p, 0, 0, 0)),
            pl.BlockSpec((1, LANE, TQ), lambda b, p, q: (b, p, q)),
        ],
        out_specs=pl.BlockSpec((1, TQ, LANE), lambda b, p, q: (b, q, p)),
        out_shape=jax.ShapeDtypeStruct((B, Lp, D), BF16),
        scratch_shapes=[
            pltpu.VMEM((2, pair, CK, TQ), F32),
            pltpu.VMEM((2, pair, 1, TQ), F32),
            pltpu.VMEM((pair, 1, TQ), F32),
            pltpu.VMEM((pair, vt.shape[3], TQ), F32),
        ],
        compiler_params=_cparams("arbitrary", "arbitrary", "arbitrary"),
        name="fox_attn",
    )(bound.reshape(1).astype(F32), qt, k, vt, gt)


def _ffn_kernel(h_ref, x_ref, wo_ref, g_ref, wg_ref, wu_ref, cw_ref, wd_ref, o_ref, carry_ref, *stage,
                skip_rows=0, keep_rows=0, n_b=0, n_t=0):
    @pl.when(pl.program_id(1) == 0)
    def _():
        carry_ref[...] = jnp.zeros_like(carry_ref)

    T = h_ref.shape[1]
    F = wg_ref.shape[1]
    S = FFN_SUB
    n_chunks = F // FFN_CHUNK
    row = lax.broadcasted_iota(jnp.int32, (S, FFN_CHUNK), 0)

    if stage:
        buf_ref, sem_ref = stage
        n_steps = n_b * n_t
        step = pl.program_id(0) * n_t + pl.program_id(1)
        slot = step % 2
        last_rows = skip_rows + keep_rows - (n_t - 1) * T

        def copy_out(st, start):
            b, t = st // n_t, st % n_t
            sl = st % 2

            def go(src, dst):
                cp = pltpu.make_async_copy(buf_ref.at[sl, src], o_ref.at[b, dst], sem_ref.at[sl])
                cp.start() if start else cp.wait()

            @pl.when(t == 0)
            def _():
                go(pl.ds(skip_rows, T - skip_rows), pl.ds(0, T - skip_rows))

            @pl.when(jnp.logical_and(t > 0, t < n_t - 1))
            def _():
                go(pl.ds(0, T), pl.ds(pl.multiple_of(t * T - skip_rows, SUBLANE), T))

            @pl.when(jnp.logical_and(t == n_t - 1, t > 0))
            def _():
                go(pl.ds(0, last_rows), pl.ds(keep_rows - last_rows, last_rows))

        @pl.when(step >= 2)
        def _():
            copy_out(step - 2, start=False)

    h1s = [h_ref[0, r * S:(r + 1) * S] + _dot(x_ref[0, r * S:(r + 1) * S], wo_ref[...]) for r in range(T // S)]
    xns = [_rms(h1, g_ref[...]).astype(BF16) for h1 in h1s]
    tails = [carry_ref[:, c * FFN_CHUNK:(c + 1) * FFN_CHUNK] for c in range(n_chunks)]

    for r, xn in enumerate(xns):
        def gate_up(c):
            sl = slice(c * FFN_CHUNK, (c + 1) * FFN_CHUNK)
            return _dot(xn, wg_ref[:, sl]), _dot(xn, wu_ref[:, sl])

        acc = h1s[r]
        nxt = gate_up(0)
        for c in range(n_chunks):
            sl = slice(c * FFN_CHUNK, (c + 1) * FFN_CHUNK)
            a, u = nxt
            if c + 1 < n_chunks:
                nxt = gate_up(c + 1)
            prev = tails[c]
            a1 = jnp.where(row == 0, prev[SUBLANE - 1:SUBLANE], pltpu.roll(a, 1, 0))
            a2 = jnp.where(row == 0, prev[SUBLANE - 2:SUBLANE - 1],
                           jnp.where(row == 1, prev[SUBLANE - 1:SUBLANE], pltpu.roll(a, 2, 0)))
            tails[c] = a[S - SUBLANE:S]
            cw = cw_ref[:, sl]
            z = cw[3:4] + cw[0:1] * a2 + cw[1:2] * a1 + cw[2:3] * a
            act = (z * jax.nn.sigmoid(z) * u).astype(BF16)
            acc = acc + _dot(act, wd_ref[sl, :])
        if stage:
            stage[0][slot, r * S:(r + 1) * S] = acc
        else:
            o_ref[0, r * S:(r + 1) * S] = acc

    for c in range(n_chunks):
        carry_ref[:, c * FFN_CHUNK:(c + 1) * FFN_CHUNK] = tails[c]

    if stage:
        copy_out(step, start=True)

        @pl.when(step == n_steps - 1)
        def _():
            if n_steps > 1:
                copy_out(step - 1, start=False)
            copy_out(step, start=False)


def _ffn(h, x, w_out, gain, w_gate, w_up, conv_w, conv_b, w_down, keep=None):
    B, Lp, D = h.shape
    F = w_gate.shape[1]
    cw = jnp.concatenate([conv_w, conv_b[None], jnp.zeros((SUBLANE - conv_w.shape[0] - 1, F), F32)], axis=0)
    if keep is None:
        kern = _ffn_kernel
        out_spec = pl.BlockSpec((1, FFN_ROWS, D), lambda b, t: (b, t, 0))
        out_shape = jax.ShapeDtypeStruct((B, Lp, D), F32)
        staging = []
    else:
        skip_rows, keep_rows = keep
        assert skip_rows % SUBLANE == 0 and 0 < skip_rows < FFN_ROWS and Lp // FFN_ROWS >= 2
        assert (Lp // FFN_ROWS - 1) * FFN_ROWS < skip_rows + keep_rows <= Lp
        kern = functools.partial(_ffn_kernel, skip_rows=skip_rows, keep_rows=keep_rows, n_b=B, n_t=Lp // FFN_ROWS)
        out_spec = pl.BlockSpec(memory_space=pl.ANY)
        out_shape = jax.ShapeDtypeStruct((B, keep_rows, D), F32)
        staging = [pltpu.VMEM((2, FFN_ROWS, D), F32), pltpu.SemaphoreType.DMA((2,))]
    return pl.pallas_call(
        kern,
        grid=(B, Lp // FFN_ROWS),
        in_specs=[
            pl.BlockSpec((1, FFN_ROWS, D), lambda b, t: (b, t, 0)),
            pl.BlockSpec((1, FFN_ROWS, D), lambda b, t: (b, t, 0)),
            _resident((D, D)), _resident((1, D)), _resident((D, F)), _resident((D, F)),
            _resident((SUBLANE, F)), _resident((F, D)),
        ],
        out_specs=out_spec,
        out_shape=out_shape,
        scratch_shapes=[pltpu.VMEM((SUBLANE, F), F32)] + staging,
        compiler_params=_cparams("arbitrary", "arbitrary"),
        name="ffn",
    )(h, x, w_out, gain.reshape(1, D), w_gate, w_up, cw, w_down)


def _hgrn_in_kernel(h_ref, g_ref, w_ref, lbs_ref, q_ref, lf_ref, v_ref, sg_ref, *, layer):
    R, D = h_ref.shape
    T = TOK
    lbs = lbs_ref[...]
    e = jnp.exp(lbs - jnp.max(lbs, axis=0, keepdims=True))
    p = e / jnp.sum(e, axis=0, keepdims=True)
    lb = jnp.sum(p[1:layer + 1], axis=0, keepdims=True) if layer > 0 else jnp.zeros((1, D), F32)

    xns = [_rms(h_ref[r * T:(r + 1) * T], g_ref[...]).astype(BF16) for r in range(R // T)]
    def project(xn):
        return [_dot(xn, w_ref[:, i * D:(i + 1) * D]) for i in range(4)]

    nxt = project(xns[0])
    for r in range(len(xns)):
        rows = slice(r * T, (r + 1) * T)
        yq, yf, yi, yg = nxt
        if r + 1 < len(xns):
            nxt = project(xns[r + 1])
        q_ref[rows] = (yq * jax.nn.sigmoid(yq)).astype(q_ref.dtype)
        f = lb + (1.0 - lb) * jax.nn.sigmoid(yf)
        lf_ref[rows] = jnp.log(f)
        v_ref[rows] = yi.astype(v_ref.dtype)
        sg_ref[rows] = jax.nn.sigmoid(yg).astype(sg_ref.dtype)


def _hgrn_in(h, gain, w, lower_bounds, layer):
    B, Lp, D = h.shape
    rows = B * Lp
    blk = pl.BlockSpec((HGRN_IN_ROWS, D), lambda i: (i, 0))
    outs = pl.pallas_call(
        functools.partial(_hgrn_in_kernel, layer=layer),
        grid=(rows // HGRN_IN_ROWS,),
        in_specs=[blk, _resident((1, D)), _resident(w.shape), _resident(lower_bounds.shape)],
        out_specs=[blk, blk, blk, blk],
        out_shape=[
            jax.ShapeDtypeStruct((rows, D), BF16),
            jax.ShapeDtypeStruct((rows, D), F32),
            jax.ShapeDtypeStruct((rows, D), BF16),
            jax.ShapeDtypeStruct((rows, D), BF16),
        ],
        compiler_params=_cparams("arbitrary"),
        name="hgrn_in",
    )(h.reshape(rows, D), gain.reshape(1, D), w, lower_bounds)
    return [o.reshape(B, Lp, D) for o in outs]


def _hgrn_rec_kernel(q_ref, lf_ref, v_ref, sg_ref, og_ref, o_ref, st_ref, stn_ref, *, dk):
    @pl.when(pl.program_id(1) == 0)
    def _():
        st_ref[...] = jnp.zeros_like(st_ref)

    C = HGRN_CHUNK
    R, D = q_ref.shape[1], q_ref.shape[2]
    sub = HGRN_SUB
    nsub = C // sub
    r_cc = lax.broadcasted_iota(jnp.int32, (C, C), 0)
    c_cc = lax.broadcasted_iota(jnp.int32, (C, C), 1)
    causal = c_cc <= r_cc
    ltri = jnp.where(causal, 1.0, 0.0).astype(BF16)
    row = lax.broadcasted_iota(jnp.int32, (C, dk), 0)

    def run(chunk_factored, row0, nrows, src_ref):
        if not chunk_factored:
            diag_mask = jnp.logical_and(c_cc >= r_cc - (r_cc & (sub - 1)), causal)
            d_sel = (lax.broadcasted_iota(jnp.int32, (sub * dk, C), 0)
                     - (lax.broadcasted_iota(jnp.int32, (sub * dk, C), 1) & (sub - 1)) * dk)
            sel = jnp.where(jnp.logical_and(d_sel >= 0, d_sel < dk), 1.0, 0.0).astype(BF16)

        units = [(hh, ci) for hh in range(D // dk) for ci in range(nrows // C)]

        def rows_of(ci):
            return pl.ds(row0 + ci * C, C)

        def blk(ref, u):
            hh, ci = u
            return ref[0, rows_of(ci), hh * dk:(hh + 1) * dk]

        b = {}
        for u in units:
            hi, mid, lo = _split3(blk(lf_ref, u))
            b[u] = (_dot(ltri, hi) + _dot(ltri, mid) + _dot(ltri, lo)) * LOG2E

        qe, kk, ke, scores = {}, {}, {}, {}
        for u in units:
            q = blk(q_ref, u).astype(F32)
            kk[u] = 1.0 - jnp.exp(blk(lf_ref, u))
            qe[u] = (q * jnp.exp2(b[u])).astype(BF16)
            if chunk_factored:
                ke[u] = kk[u] * jnp.exp2(-b[u])
                scores[u] = _dot_nt(qe[u], ke[u].astype(BF16))
            else:
                parts = [jnp.zeros((sub, C), F32)]
                for i in range(1, nsub):
                    bref = b[u][i * sub - 1:i * sub]
                    qi = q[i * sub:(i + 1) * sub] * jnp.exp2(b[u][i * sub:(i + 1) * sub] - bref)
                    ks = kk[u] * jnp.exp2(jnp.where(row < i * sub, bref - b[u], NEG))
                    parts.append(_dot_nt(qi.astype(BF16), ks.astype(BF16)))
                scores[u] = jnp.concatenate(parts, axis=0)

        intra, kv = {}, {}
        for u in units:
            if chunk_factored:
                a = jnp.where(causal, scores[u], 0.0)
                ke_end = ke[u] * jnp.exp2(b[u][C - 1:C])
            else:
                b4 = b[u].reshape(nsub, sub, dk)
                q4 = blk(q_ref, u).astype(F32).reshape(nsub, sub, dk)
                k4 = kk[u].reshape(nsub, sub, dk)
                z = [(q4 * jnp.exp2(jnp.minimum(b4 - b4[:, s:s + 1], 0.0)) * k4[:, s:s + 1])
                     .reshape(C, dk).astype(BF16) for s in range(sub)]
                a = jnp.where(diag_mask, _dot(jnp.concatenate(z, axis=1), sel), scores[u])
                ke_end = kk[u] * jnp.exp2(b[u][C - 1:C] - b[u])
            v = blk(v_ref, u)
            intra[u] = _dot(a.astype(BF16), v)
            kv[u] = _dot_tn(v, ke_end.astype(BF16))

        out = {}
        for hh in range(D // dk):
            st = src_ref[hh]
            for ci in range(nrows // C):
                u = (hh, ci)
                out[u] = _dot_nt(qe[u], st.astype(BF16)) + intra[u]
                st = st * jnp.exp2(b[u][C - 1:C]) + kv[u]
            stn_ref[hh] = st

        for u in units:
            hh, ci = u
            on = _rms(out[u], og_ref[...]) * blk(sg_ref, u).astype(F32)
            o_ref[0, rows_of(ci), hh * dk:(hh + 1) * dk] = on.astype(o_ref.dtype)

    worst = -jnp.min(jnp.sum(lf_ref[0].reshape(R // C, C, D), axis=1))
    run(True, 0, R, st_ref)

    @pl.when(worst > HGRN_SAFE_DECAY)
    def _():
        stn_ref[...] = st_ref[...]

        def piece(i, _):
            run(False, pl.multiple_of(i * HGRN_RERUN_ROWS, HGRN_RERUN_ROWS), HGRN_RERUN_ROWS, stn_ref)
            return 0

        lax.fori_loop(0, R // HGRN_RERUN_ROWS, piece, 0)

    st_ref[...] = stn_ref[...]


def _hgrn_rec(q, lf, v, sg, o_gain):
    B, Lp, D = q.shape
    dk = o_gain.shape[0]
    blk = pl.BlockSpec((1, HGRN_ROWS, D), lambda b, t: (b, t, 0))
    return pl.pallas_call(
        functools.partial(_hgrn_rec_kernel, dk=dk),
        grid=(B, Lp // HGRN_ROWS),
        in_specs=[blk, blk, blk, blk, _resident((1, dk))],
        out_specs=blk,
        out_shape=jax.ShapeDtypeStruct((B, Lp, D), BF16),
        scratch_shapes=[pltpu.VMEM((D // dk, dk, dk), F32)] * 2,
        compiler_params=_cparams("arbitrary", "arbitrary"),
        name="hgrn_rec",
    )(q, lf, v, sg, o_gain.reshape(1, dk))


def kernel(x, meta_tokens, fox_norm, fox_w_in, fox_b_f, fox_q_gain, fox_k_gain, fox_w_out,
           hgrn_norm, hgrn_w_in, hgrn_lower_bounds, hgrn_o_gain, hgrn_w_out,
           ffn_norm, ffn_w_gate, ffn_w_up, ffn_conv_w, ffn_conv_b, ffn_w_down):
    B, S, D = x.shape
    L = S + N_META
    Lp = -(-L // TOK) * TOK
    depth = ffn_norm.shape[0]
    heads = fox_b_f.shape[1]
    hd = D // heads
    assert LANE % hd == 0 and hd + N_EXTRA <= LANE and D % LANE == 0
    assert Lp % FOX_ROWS == 0 and (B * Lp) % HGRN_IN_ROWS == 0 and HGRN_IN_ROWS % TOK == 0
    assert Lp % FFN_ROWS == 0 and FFN_ROWS % FFN_SUB == 0 and Lp % HGRN_ROWS == 0 and HGRN_ROWS % HGRN_RERUN_ROWS == 0 and HGRN_RERUN_ROWS % HGRN_CHUNK == 0 and (Lp // TOK) % KEY_BLOCKS == 0 and (Lp // TOK) % Q_BLOCKS == 0 and KEY_BLOCKS % Q_BLOCKS == 0

    meta = jnp.broadcast_to(meta_tokens[None].astype(x.dtype), (B, N_META, D))
    h = jnp.concatenate([meta, x, jnp.zeros((B, Lp - L, D), x.dtype)], axis=1)

    fox_w_in, fox_w_out, hgrn_w_in, hgrn_w_out, ffn_w_gate, ffn_w_up, ffn_w_down = (
        w.astype(BF16) for w in (fox_w_in, fox_w_out, hgrn_w_in, hgrn_w_out, ffn_w_gate, ffn_w_up, ffn_w_down))

    for layer in range(depth):
        j = layer // 2
        if layer % 2 == 0:
            w_t = fox_w_in[j].T
            qt, k, vt, gt = _fox_in(h, fox_norm[j], w_t, fox_b_f[j],
                                    fox_q_gain[j], fox_k_gain[j], heads)
            k = k.reshape(B, heads, -1, KEY_BLOCKS * TOK, LANE)
            og, w_out = _fox_attn(qt, k, vt, gt, fox_q_gain[j], fox_k_gain[j], hd), fox_w_out[j]
        else:
            q, lf, v, sg = _hgrn_in(h, hgrn_norm[j], hgrn_w_in[j], hgrn_lower_bounds, layer)
            og, w_out = _hgrn_rec(q, lf, v, sg, hgrn_o_gain[j]), hgrn_w_out[j]
        h = _ffn(h, og, w_out, ffn_norm[layer], ffn_w_gate[layer], ffn_w_up[layer],
                 ffn_conv_w[layer], ffn_conv_b[layer], ffn_w_down[layer],
                 keep=(N_META, S) if layer == depth - 1 else None)
    return h
```

```python
import functools

import jax
import jax.numpy as jnp
from jax import lax
from jax.experimental import pallas as pl
from jax.experimental.pallas import tpu as pltpu

N_META = 16
EPS = 1e-6
NEG = -1e30
LANE = 128
SUBLANE = 8
TOK = 256
KEY_BLOCKS = 3
Q_BLOCKS = 3
FOX_ROWS = 768
FFN_ROWS = 768
FFN_SUB = 256
FFN_CHUNK = 256
HGRN_IN_ROWS = 768
HGRN_CHUNK = 64
HGRN_SUB = 16
HGRN_ROWS = 256
HGRN_RERUN_ROWS = 128
HGRN_SAFE_DECAY = 60.0
LOG2E = 1.4426950408889634
N_EXTRA = 16
ATT_TRIP = 4
ATT_SAFE_GAP = 96.0
VMEM_LIMIT = 56 * 1024 * 1024

F32 = jnp.float32
BF16 = jnp.bfloat16


def _cparams(*sem):
    return pltpu.CompilerParams(dimension_semantics=sem, vmem_limit_bytes=VMEM_LIMIT)


def _resident(shape):
    zeros = (0,) * len(shape)
    return pl.BlockSpec(shape, lambda *_: zeros, pipeline_mode=pl.Buffered(1))


def _resident_layer(stacked, layer):
    index = (layer,) + (0,) * (stacked.ndim - 1)
    return pl.BlockSpec((None,) + stacked.shape[1:], lambda *_: index, pipeline_mode=pl.Buffered(1))


def _split3(x):
    hi = x.astype(BF16)
    r = x - hi.astype(F32)
    mid = r.astype(BF16)
    lo = (r - mid.astype(F32)).astype(BF16)
    return hi, mid, lo


def _dot(a, b):
    return jnp.dot(a, b, preferred_element_type=F32)


def _dot_nt(a, b):
    return lax.dot_general(a, b, (((1,), (1,)), ((), ())), preferred_element_type=F32)


def _dot_tn(a, b):
    return lax.dot_general(a, b, (((0,), (0,)), ((), ())), preferred_element_type=F32)


def _rms(x, gain):
    ms = jnp.mean(x * x, axis=-1, keepdims=True)
    return x * lax.rsqrt(ms + EPS) * gain


def _fox_in_kernel(h_ref, g_ref, w_ref, bf_ref, qg_ref, kg_ref,
                   qt_ref, k_ref, vt_ref, gt_ref, carry_ref, *, heads):
    @pl.when(pl.program_id(1) == 0)
    def _():
        carry_ref[...] = jnp.zeros_like(carry_ref)

    R, D = h_ref.shape[1], h_ref.shape[2]
    T = TOK
    hd = D // heads
    xnts = [_rms(h_ref[0, r * T:(r + 1) * T], g_ref[...]).T.astype(BF16) for r in range(R // T)]

    src = lax.broadcasted_iota(jnp.int32, (T, T), 0)
    dst = lax.broadcasted_iota(jnp.int32, (T, T), 1)
    upper = jnp.where(src <= dst, 1.0, 0.0).astype(BF16)
    row = lax.broadcasted_iota(jnp.int32, (N_EXTRA, T), 0)
    zpad = jnp.zeros((LANE - hd - N_EXTRA, T), F32)
    ones_row = jnp.where(row == 0, 1.0, 0.0)
    carry = carry_ref[:, 0:1]

    def head_norm(y, gain_ref, scale):
        y = y.reshape(heads, hd, T)
        ms = jnp.mean(y * y, axis=1, keepdims=True)
        return y * lax.rsqrt(ms + EPS) * (gain_ref[...] * scale)

    def project(xnt):
        f0 = 3 * D
        return ([_dot(w_ref[i * D:(i + 1) * D, :], xnt) for i in range(3)]
                + [_dot(w_ref[f0:f0 + heads, :], xnt), _dot(w_ref[f0 + heads:f0 + heads + D, :], xnt)])

    nxt = project(xnts[0])
    for r in range(len(xnts)):
        cols = slice(r * T, (r + 1) * T)
        yq, yk, yv, fl, yg = nxt
        if r + 1 < len(xnts):
            nxt = project(xnts[r + 1])

        fl = fl + bf_ref[...]
        lf = jnp.minimum(fl, 0.0) - jnp.log(1.0 + jnp.exp(-jnp.abs(fl)))
        hi, mid, lo = _split3(lf)
        c = carry + (_dot(hi, upper) + _dot(mid, upper) + _dot(lo, upper))
        carry = carry + jnp.sum(lf, axis=1, keepdims=True)
        chi, cmid, clo = (p.astype(F32) for p in _split3(c * LOG2E))

        qn = head_norm(yq, qg_ref, hd ** -0.5 * LOG2E)
        kn = head_norm(yk, kg_ref, 1.0)
        vv = yv.reshape(heads, hd, T)
        gt_ref[0, :, cols] = jax.nn.sigmoid(yg).astype(gt_ref.dtype)

        for hh in range(heads):
            a, b, d = (jnp.broadcast_to(p[hh:hh + 1], (N_EXTRA, T)) for p in (chi, cmid, clo))
            eq = jnp.where(row == 0, a, jnp.where(row == 1, b, jnp.where(row == 2, d, jnp.where(row < 6, 1.0, 0.0))))
            ek = jnp.where(row < 3, 1.0, jnp.where(row == 3, -a, jnp.where(row == 4, -b, jnp.where(row == 5, -d, 0.0))))
            qt_ref[0, hh, 0, :, cols] = jnp.concatenate([qn[hh], eq, zpad], axis=0).astype(qt_ref.dtype)
            k_ref[0, hh, r] = jnp.concatenate([kn[hh], ek, zpad], axis=0).T.astype(k_ref.dtype)
            vt_ref[0, hh, 0, :, cols] = jnp.concatenate([vv[hh], ones_row], axis=0).astype(vt_ref.dtype)

    carry_ref[...] = jnp.broadcast_to(carry, carry_ref.shape)


def _fox_in(h, gain, w_t, b_f, q_gain, k_gain, heads):
    B, Lp, D = h.shape
    hd = D // heads
    R = FOX_ROWS
    nR, sub = Lp // R, R // TOK
    assert R == Q_BLOCKS * TOK == KEY_BLOCKS * TOK
    return pl.pallas_call(
        functools.partial(_fox_in_kernel, heads=heads),
        grid=(B, nR),
        in_specs=[
            pl.BlockSpec((1, R, D), lambda b, t: (b, t, 0)),
            _resident((1, D)), _resident(w_t.shape),
            _resident((heads, 1)), _resident((hd, 1)), _resident((hd, 1)),
        ],
        out_specs=[
            pl.BlockSpec((1, heads, 1, LANE, R), lambda b, t: (b, 0, t, 0, 0)),
            pl.BlockSpec((1, heads, sub, TOK, LANE), lambda b, t: (b, 0, t, 0, 0)),
            pl.BlockSpec((1, heads, 1, hd + N_EXTRA, R), lambda b, t: (b, 0, t, 0, 0)),
            pl.BlockSpec((1, D, R), lambda b, t: (b, 0, t)),
        ],
        out_shape=[
            jax.ShapeDtypeStruct((B, heads, nR, LANE, R), BF16),
            jax.ShapeDtypeStruct((B, heads, nR * sub, TOK, LANE), BF16),
            jax.ShapeDtypeStruct((B, heads, nR, hd + N_EXTRA, R), BF16),
            jax.ShapeDtypeStruct((B, D, Lp), BF16),
        ],
        scratch_shapes=[pltpu.VMEM((heads, LANE), F32)],
        compiler_params=_cparams("arbitrary", "arbitrary"),
        name="fox_in",
    )(h, gain.reshape(1, D), w_t, b_f.reshape(heads, 1),
      q_gain.reshape(hd, 1), k_gain.reshape(hd, 1))


def _fox_attn_kernel(bound_ref, qt_ref, k_ref, vt_ref, gt_ref, o_ref, s_ref, cm_ref, m_ref, acc_ref, *, hd):
    qi = pl.program_id(2)
    T = qt_ref.shape[-1]
    pair = qt_ref.shape[1]
    CK = k_ref.shape[3]
    n_full = (qi * T) // CK
    assert CK == T
    bound = bound_ref[0]

    def logits(c, j, masked):
        s = _dot(k_ref[0, j, c], qt_ref[0, j, 0])
        if masked:
            key = c * CK - qi * T + lax.broadcasted_iota(jnp.int32, (CK, T), 0)
            qry = lax.broadcasted_iota(jnp.int32, (CK, T), 1)
            s = jnp.where(key <= qry, s, NEG)
        return s

    def finalize():
        o = jnp.concatenate([acc_ref[j, :hd] / acc_ref[j, hd:hd + 1] for j in range(pair)], axis=0)
        o_ref[0] = (o * gt_ref[0].astype(F32)).T.astype(o_ref.dtype)

    def bounded():
        acc_ref[...] = jnp.zeros(acc_ref.shape, F32)

        def diag_logits(c, j):
            parts = []
            for lo in range(0, CK, TOK):
                s = _dot(k_ref[0, j, c, lo:lo + TOK, :], qt_ref[0, j, 0, :, lo:])
                key = lax.broadcasted_iota(jnp.int32, s.shape, 0)
                qry = lax.broadcasted_iota(jnp.int32, s.shape, 1)
                parts.append(jnp.where(key <= qry, s, NEG))
            return parts

        def sweep(chunks):
            units = [(c, j, diag) for c, diag in chunks for j in range(pair)]
            ahead = 2

            def unit_logits(c, j, diag):
                return diag_logits(c, j) if diag else logits(c, j, False)

            s = [unit_logits(*u) for u in units[:ahead]]
            for n, (c, j, diag) in enumerate(units):
                if n + ahead < len(units):
                    s.append(unit_logits(*units[n + ahead]))
                if diag:
                    for part, lo in zip(s[n], range(0, CK, TOK)):
                        p = jnp.exp2(part - bound).astype(BF16)
                        acc_ref[j, :, lo:] = acc_ref[j, :, lo:] + _dot(vt_ref[0, j, c, :, lo:lo + TOK], p)
                else:
                    p = jnp.exp2(s[n] - bound).astype(BF16)
                    acc_ref[j] = acc_ref[j] + _dot(vt_ref[0, j, c], p)

        rem = n_full % ATT_TRIP

        def body(i, _):
            sweep([(ATT_TRIP * i + d, False) for d in range(ATT_TRIP)])
            return 0

        lax.fori_loop(0, n_full // ATT_TRIP, body, 0)

        for r in range(ATT_TRIP):
            @pl.when(rem == r)
            def _(r=r):
                sweep([(n_full - r + d, False) for d in range(r)] + [(n_full, True)])
                finalize()

    def scores(c, slot, masked=False):
        for j in range(pair):
            s = logits(c, j, masked)
            s_ref[slot, j] = s
            cm_ref[slot, j] = jnp.max(s, axis=0, keepdims=True)

    def accumulate(c, slot):
        for j in range(pair):
            m = m_ref[j]
            m_new = jnp.maximum(m, cm_ref[slot, j])
            p = jnp.exp2(s_ref[slot, j] - m_new).astype(BF16)
            acc_ref[j] = acc_ref[j] * jnp.exp2(m - m_new) + _dot(vt_ref[0, j, c], p)
            m_ref[j] = m_new

    def online():
        m_ref[...] = jnp.full(m_ref.shape, NEG, F32)
        acc_ref[...] = jnp.zeros(acc_ref.shape, F32)

        steps = n_full - 1
        odd = steps % 2

        @pl.when(n_full == 0)
        def _():
            scores(0, 1, masked=True)

        @pl.when(jnp.logical_and(n_full > 0, odd == 0))
        def _():
            scores(0, 0)

        @pl.when(jnp.logical_and(n_full > 0, odd == 1))
        def _():
            scores(0, 1)
            scores(1, 0)
            accumulate(0, 1)

        def body(i, _):
            c = odd + 2 * i
            scores(c + 1, 1)
            accumulate(c, 0)
            scores(c + 2, 0)
            accumulate(c + 1, 1)
            return 0

        lax.fori_loop(0, steps // 2, body, 0)

        @pl.when(n_full > 0)
        def _():
            scores(n_full, 1, masked=True)
            accumulate(n_full - 1, 0)

        accumulate(n_full, 1)
        finalize()

    bounded()

    @pl.when(2.0 * bound > ATT_SAFE_GAP)
    def _():
        online()


def _fox_attn(qt, k, vt, gt, q_gain, k_gain, hd):
    B, heads, nQ = qt.shape[:3]
    TQ = qt.shape[-1]
    nC, CK = k.shape[2], k.shape[3]
    D, Lp = gt.shape[1], gt.shape[2]
    pair = LANE // hd
    bound = jnp.max(jnp.abs(q_gain)) * jnp.max(jnp.abs(k_gain)) * (hd ** 0.5 * LOG2E)
    return pl.pallas_call(
        functools.partial(_fox_attn_kernel, hd=hd),
        grid=(B, heads // pair, nQ),
        in_specs=[
            pl.BlockSpec(memory_space=pltpu.SMEM),
            pl.BlockSpec((1, pair, 1, LANE, TQ), lambda b, p, q: (b, p, q, 0, 0)),
            pl.BlockSpec((1, pair, nC, CK, LANE), lambda b, p, q: (b, p, 0, 0, 0)),
            pl.BlockSpec((1, pair, nC, vt.shape[3], CK), lambda b, p, q: (b, p, 0, 0, 0)),
            pl.BlockSpec((1, LANE, TQ), lambda b, p, q: (b, p, q)),
        ],
        out_specs=pl.BlockSpec((1, TQ, LANE), lambda b, p, q: (b, q, p)),
        out_shape=jax.ShapeDtypeStruct((B, Lp, D), BF16),
        scratch_shapes=[
            pltpu.VMEM((2, pair, CK, TQ), F32),
            pltpu.VMEM((2, pair, 1, TQ), F32),
            pltpu.VMEM((pair, 1, TQ), F32),
            pltpu.VMEM((pair, vt.shape[3], TQ), F32),
        ],
        compiler_params=_cparams("arbitrary", "arbitrary", "arbitrary"),
        name="fox_attn",
    )(bound.reshape(1).astype(F32), qt, k, vt, gt)


def _ffn_kernel(h_ref, x_ref, wo_ref, g_ref, wg_ref, wu_ref, cw_ref, wd_ref, o_ref, carry_ref, *stage,
                skip_rows=0, keep_rows=0, n_b=0, n_t=0):
    @pl.when(pl.program_id(1) == 0)
    def _():
        carry_ref[...] = jnp.zeros_like(carry_ref)

    T = h_ref.shape[1]
    F = wg_ref.shape[1]
    S = FFN_SUB
    n_chunks = F // FFN_CHUNK
    row = lax.broadcasted_iota(jnp.int32, (S, FFN_CHUNK), 0)

    if stage:
        buf_ref, sem_ref = stage
        n_steps = n_b * n_t
        step = pl.program_id(0) * n_t + pl.program_id(1)
        slot = step % 2
        last_rows = skip_rows + keep_rows - (n_t - 1) * T

        def copy_out(st, start):
            b, t = st // n_t, st % n_t
            sl = st % 2

            def go(src, dst):
                cp = pltpu.make_async_copy(buf_ref.at[sl, src], o_ref.at[b, dst], sem_ref.at[sl])
                cp.start() if start else cp.wait()

            @pl.when(t == 0)
            def _():
                go(pl.ds(skip_rows, T - skip_rows), pl.ds(0, T - skip_rows))

            @pl.when(jnp.logical_and(t > 0, t < n_t - 1))
            def _():
                go(pl.ds(0, T), pl.ds(pl.multiple_of(t * T - skip_rows, SUBLANE), T))

            @pl.when(jnp.logical_and(t == n_t - 1, t > 0))
            def _():
                go(pl.ds(0, last_rows), pl.ds(keep_rows - last_rows, last_rows))

        @pl.when(step >= 2)
        def _():
            copy_out(step - 2, start=False)

    h1s = [h_ref[0, r * S:(r + 1) * S] + _dot(x_ref[0, r * S:(r + 1) * S], wo_ref[...]) for r in range(T // S)]
    xns = [_rms(h1, g_ref[...]).astype(BF16) for h1 in h1s]
    tails = [carry_ref[:, c * FFN_CHUNK:(c + 1) * FFN_CHUNK] for c in range(n_chunks)]

    for r, xn in enumerate(xns):
        def gate_up(c):
            sl = slice(c * FFN_CHUNK, (c + 1) * FFN_CHUNK)
            return _dot(xn, wg_ref[:, sl]), _dot(xn, wu_ref[:, sl])

        acc = h1s[r]
        nxt = gate_up(0)
        for c in range(n_chunks):
            sl = slice(c * FFN_CHUNK, (c + 1) * FFN_CHUNK)
            a, u = nxt
            if c + 1 < n_chunks:
                nxt = gate_up(c + 1)
            prev = tails[c]
            a1 = jnp.where(row == 0, prev[SUBLANE - 1:SUBLANE], pltpu.roll(a, 1, 0))
            a2 = jnp.where(row == 0, prev[SUBLANE - 2:SUBLANE - 1],
                           jnp.where(row == 1, prev[SUBLANE - 1:SUBLANE], pltpu.roll(a, 2, 0)))
            tails[c] = a[S - SUBLANE:S]
            cw = cw_ref[:, sl]
            z = cw[3:4] + cw[0:1] * a2 + cw[1:2] * a1 + cw[2:3] * a
            act = (z * jax.nn.sigmoid(z) * u).astype(BF16)
            acc = acc + _dot(act, wd_ref[sl, :])
        if stage:
            stage[0][slot, r * S:(r + 1) * S] = acc
        else:
            o_ref[0, r * S:(r + 1) * S] = acc

    for c in range(n_chunks):
        carry_ref[:, c * FFN_CHUNK:(c + 1) * FFN_CHUNK] = tails[c]

    if stage:
        copy_out(step, start=True)

        @pl.when(step == n_steps - 1)
        def _():
            if n_steps > 1:
                copy_out(step - 1, start=False)
            copy_out(step, start=False)


def _ffn(h, x, w_out, j, gain, w_gate, w_up, conv_w, conv_b, w_down, layer, keep=None):
    B, Lp, D = h.shape
    F = w_gate.shape[2]
    cw = jnp.concatenate([conv_w, conv_b[None], jnp.zeros((SUBLANE - conv_w.shape[0] - 1, F), F32)], axis=0)
    if keep is None:
        kern = _ffn_kernel
        out_spec = pl.BlockSpec((1, FFN_ROWS, D), lambda b, t: (b, t, 0))
        out_shape = jax.ShapeDtypeStruct((B, Lp, D), F32)
        staging = []
    else:
        skip_rows, keep_rows = keep
        assert skip_rows % SUBLANE == 0 and 0 < skip_rows < FFN_ROWS and Lp // FFN_ROWS >= 2
        assert (Lp // FFN_ROWS - 1) * FFN_ROWS < skip_rows + keep_rows <= Lp
        kern = functools.partial(_ffn_kernel, skip_rows=skip_rows, keep_rows=keep_rows, n_b=B, n_t=Lp // FFN_ROWS)
        out_spec = pl.BlockSpec(memory_space=pl.ANY)
        out_shape = jax.ShapeDtypeStruct((B, keep_rows, D), F32)
        staging = [pltpu.VMEM((2, FFN_ROWS, D), F32), pltpu.SemaphoreType.DMA((2,))]
    return pl.pallas_call(
        kern,
        grid=(B, Lp // FFN_ROWS),
        in_specs=[
            pl.BlockSpec((1, FFN_ROWS, D), lambda b, t: (b, t, 0)),
            pl.BlockSpec((1, FFN_ROWS, D), lambda b, t: (b, t, 0)),
            _resident_layer(w_out, j), _resident((1, D)), _resident_layer(w_gate, layer), _resident_layer(w_up, layer),
            _resident((SUBLANE, F)), _resident_layer(w_down, layer),
        ],
        out_specs=out_spec,
        out_shape=out_shape,
        scratch_shapes=[pltpu.VMEM((SUBLANE, F), F32)] + staging,
        compiler_params=_cparams("arbitrary", "arbitrary"),
        name="ffn",
    )(h, x, w_out, gain.reshape(1, D), w_gate, w_up, cw, w_down)


def _hgrn_in_kernel(h_ref, g_ref, w_ref, lbs_ref, q_ref, lf_ref, v_ref, sg_ref, *, layer):
    R, D = h_ref.shape
    T = TOK
    lbs = lbs_ref[...]
    e = jnp.exp(lbs - jnp.max(lbs, axis=0, keepdims=True))
    p = e / jnp.sum(e, axis=0, keepdims=True)
    lb = jnp.sum(p[1:layer + 1], axis=0, keepdims=True) if layer > 0 else jnp.zeros((1, D), F32)

    xns = [_rms(h_ref[r * T:(r + 1) * T], g_ref[...]).astype(BF16) for r in range(R // T)]
    def project(xn):
        return [_dot(xn, w_ref[:, i * D:(i + 1) * D]) for i in range(4)]

    nxt = project(xns[0])
    for r in range(len(xns)):
        rows = slice(r * T, (r + 1) * T)
        yq, yf, yi, yg = nxt
        if r + 1 < len(xns):
            nxt = project(xns[r + 1])
        q_ref[rows] = (yq * jax.nn.sigmoid(yq)).astype(q_ref.dtype)
        f = lb + (1.0 - lb) * jax.nn.sigmoid(yf)
        lf_ref[rows] = jnp.log(f)
        v_ref[rows] = yi.astype(v_ref.dtype)
        sg_ref[rows] = jax.nn.sigmoid(yg).astype(sg_ref.dtype)


def _hgrn_in(h, gain, w, j, lower_bounds, layer):
    B, Lp, D = h.shape
    rows = B * Lp
    blk = pl.BlockSpec((HGRN_IN_ROWS, D), lambda i: (i, 0))
    outs = pl.pallas_call(
        functools.partial(_hgrn_in_kernel, layer=layer),
        grid=(rows // HGRN_IN_ROWS,),
        in_specs=[blk, _resident((1, D)), _resident_layer(w, j), _resident(lower_bounds.shape)],
        out_specs=[blk, blk, blk, blk],
        out_shape=[
            jax.ShapeDtypeStruct((rows, D), BF16),
            jax.ShapeDtypeStruct((rows, D), F32),
            jax.ShapeDtypeStruct((rows, D), BF16),
            jax.ShapeDtypeStruct((rows, D), BF16),
        ],
        compiler_params=_cparams("arbitrary"),
        name="hgrn_in",
    )(h.reshape(rows, D), gain.reshape(1, D), w, lower_bounds)
    return [o.reshape(B, Lp, D) for o in outs]


def _hgrn_rec_kernel(q_ref, lf_ref, v_ref, sg_ref, og_ref, o_ref, st_ref, stn_ref, *, dk):
    @pl.when(pl.program_id(1) == 0)
    def _():
        st_ref[...] = jnp.zeros_like(st_ref)

    C = HGRN_CHUNK
    R, D = q_ref.shape[1], q_ref.shape[2]
    sub = HGRN_SUB
    nsub = C // sub
    r_cc = lax.broadcasted_iota(jnp.int32, (C, C), 0)
    c_cc = lax.broadcasted_iota(jnp.int32, (C, C), 1)
    causal = c_cc <= r_cc
    ltri = jnp.where(causal, 1.0, 0.0).astype(BF16)
    row = lax.broadcasted_iota(jnp.int32, (C, dk), 0)

    def run(chunk_factored, row0, nrows, src_ref):
        if not chunk_factored:
            diag_mask = jnp.logical_and(c_cc >= r_cc - (r_cc & (sub - 1)), causal)
            d_sel = (lax.broadcasted_iota(jnp.int32, (sub * dk, C), 0)
                     - (lax.broadcasted_iota(jnp.int32, (sub * dk, C), 1) & (sub - 1)) * dk)
            sel = jnp.where(jnp.logical_and(d_sel >= 0, d_sel < dk), 1.0, 0.0).astype(BF16)

        units = [(hh, ci) for hh in range(D // dk) for ci in range(nrows // C)]

        def rows_of(ci):
            return pl.ds(row0 + ci * C, C)

        def blk(ref, u):
            hh, ci = u
            return ref[0, rows_of(ci), hh * dk:(hh + 1) * dk]

        b = {}
        for u in units:
            hi, mid, lo = _split3(blk(lf_ref, u))
            b[u] = (_dot(ltri, hi) + _dot(ltri, mid) + _dot(ltri, lo)) * LOG2E

        qe, kk, ke, scores = {}, {}, {}, {}
        for u in units:
            q = blk(q_ref, u).astype(F32)
            kk[u] = 1.0 - jnp.exp(blk(lf_ref, u))
            qe[u] = (q * jnp.exp2(b[u])).astype(BF16)
            if chunk_factored:
                ke[u] = kk[u] * jnp.exp2(-b[u])
                scores[u] = _dot_nt(qe[u], ke[u].astype(BF16))
            else:
                parts = [jnp.zeros((sub, C), F32)]
                for i in range(1, nsub):
                    bref = b[u][i * sub - 1:i * sub]
                    qi = q[i * sub:(i + 1) * sub] * jnp.exp2(b[u][i * sub:(i + 1) * sub] - bref)
                    ks = kk[u] * jnp.exp2(jnp.where(row < i * sub, bref - b[u], NEG))
                    parts.append(_dot_nt(qi.astype(BF16), ks.astype(BF16)))
                scores[u] = jnp.concatenate(parts, axis=0)

        intra, kv = {}, {}
        for u in units:
            if chunk_factored:
                a = jnp.where(causal, scores[u], 0.0)
                ke_end = ke[u] * jnp.exp2(b[u][C - 1:C])
            else:
                b4 = b[u].reshape(nsub, sub, dk)
                q4 = blk(q_ref, u).astype(F32).reshape(nsub, sub, dk)
                k4 = kk[u].reshape(nsub, sub, dk)
                z = [(q4 * jnp.exp2(jnp.minimum(b4 - b4[:, s:s + 1], 0.0)) * k4[:, s:s + 1])
                     .reshape(C, dk).astype(BF16) for s in range(sub)]
                a = jnp.where(diag_mask, _dot(jnp.concatenate(z, axis=1), sel), scores[u])
                ke_end = kk[u] * jnp.exp2(b[u][C - 1:C] - b[u])
            v = blk(v_ref, u)
            intra[u] = _dot(a.astype(BF16), v)
            kv[u] = _dot_tn(v, ke_end.astype(BF16))

        out = {}
        for hh in range(D // dk):
            st = src_ref[hh]
            for ci in range(nrows // C):
                u = (hh, ci)
                out[u] = _dot_nt(qe[u], st.astype(BF16)) + intra[u]
                st = st * jnp.exp2(b[u][C - 1:C]) + kv[u]
            stn_ref[hh] = st

        for u in units:
            hh, ci = u
            on = _rms(out[u], og_ref[...]) * blk(sg_ref, u).astype(F32)
            o_ref[0, rows_of(ci), hh * dk:(hh + 1) * dk] = on.astype(o_ref.dtype)

    worst = -jnp.min(jnp.sum(lf_ref[0].reshape(R // C, C, D), axis=1))
    run(True, 0, R, st_ref)

    @pl.when(worst > HGRN_SAFE_DECAY)
    def _():
        stn_ref[...] = st_ref[...]

        def piece(i, _):
            run(False, pl.multiple_of(i * HGRN_RERUN_ROWS, HGRN_RERUN_ROWS), HGRN_RERUN_ROWS, stn_ref)
            return 0

        lax.fori_loop(0, R // HGRN_RERUN_ROWS, piece, 0)

    st_ref[...] = stn_ref[...]


def _hgrn_rec(q, lf, v, sg, o_gain):
    B, Lp, D = q.shape
    dk = o_gain.shape[0]
    blk = pl.BlockSpec((1, HGRN_ROWS, D), lambda b, t: (b, t, 0))
    return pl.pallas_call(
        functools.partial(_hgrn_rec_kernel, dk=dk),
        grid=(B, Lp // HGRN_ROWS),
        in_specs=[blk, blk, blk, blk, _resident((1, dk))],
        out_specs=blk,
        out_shape=jax.ShapeDtypeStruct((B, Lp, D), BF16),
        scratch_shapes=[pltpu.VMEM((D // dk, dk, dk), F32)] * 2,
        compiler_params=_cparams("arbitrary", "arbitrary"),
        name="hgrn_rec",
    )(q, lf, v, sg, o_gain.reshape(1, dk))


def kernel(x, meta_tokens, fox_norm, fox_w_in, fox_b_f, fox_q_gain, fox_k_gain, fox_w_out,
           hgrn_norm, hgrn_w_in, hgrn_lower_bounds, hgrn_o_gain, hgrn_w_out,
           ffn_norm, ffn_w_gate, ffn_w_up, ffn_conv_w, ffn_conv_b, ffn_w_down):
    B, S, D = x.shape
    L = S + N_META
    Lp = -(-L // TOK) * TOK
    depth = ffn_norm.shape[0]
    heads = fox_b_f.shape[1]
    hd = D // heads
    assert LANE % hd == 0 and hd + N_EXTRA <= LANE and D % LANE == 0
    assert Lp % FOX_ROWS == 0 and (B * Lp) % HGRN_IN_ROWS == 0 and HGRN_IN_ROWS % TOK == 0
    assert Lp % FFN_ROWS == 0 and FFN_ROWS % FFN_SUB == 0 and Lp % HGRN_ROWS == 0 and HGRN_ROWS % HGRN_RERUN_ROWS == 0 and HGRN_RERUN_ROWS % HGRN_CHUNK == 0 and (Lp // TOK) % KEY_BLOCKS == 0 and (Lp // TOK) % Q_BLOCKS == 0 and KEY_BLOCKS % Q_BLOCKS == 0

    meta = jnp.broadcast_to(meta_tokens[None].astype(x.dtype), (B, N_META, D))
    h = jnp.concatenate([meta, x, jnp.zeros((B, Lp - L, D), x.dtype)], axis=1)

    fox_w_in, fox_w_out, hgrn_w_in, hgrn_w_out, ffn_w_gate, ffn_w_up, ffn_w_down = (
        w.astype(BF16) for w in (fox_w_in, fox_w_out, hgrn_w_in, hgrn_w_out, ffn_w_gate, ffn_w_up, ffn_w_down))

    for layer in range(depth):
        j = layer // 2
        if layer % 2 == 0:
            w_t = fox_w_in[j].T
            qt, k, vt, gt = _fox_in(h, fox_norm[j], w_t, fox_b_f[j],
                                    fox_q_gain[j], fox_k_gain[j], heads)
            k = k.reshape(B, heads, -1, KEY_BLOCKS * TOK, LANE)
            og, w_out = _fox_attn(qt, k, vt, gt, fox_q_gain[j], fox_k_gain[j], hd), fox_w_out
        else:
            q, lf, v, sg = _hgrn_in(h, hgrn_norm[j], hgrn_w_in, j, hgrn_lower_bounds, layer)
            og, w_out = _hgrn_rec(q, lf, v, sg, hgrn_o_gain[j]), hgrn_w_out
        h = _ffn(h, og, w_out, j, ffn_norm[layer], ffn_w_gate, ffn_w_up,
                 ffn_conv_w[layer], ffn_conv_b[layer], ffn_w_down, layer,
                 keep=(N_META, S) if layer == depth - 1 else None)
    return h
```
